```python
import jax, jax.numpy as jnp
from jax import lax
import numpy as np

D_MODEL = 1024
BATCH = 2
SEQ = 8192
DEPTH = 2
DEC_BATCH = 32
DEC_SEQ = 1
PAST_LEN = 8192
PAGE_SIZE = 128

HEAD_DIM = 64
N_ATT_HEADS = 8
N_RWKV_HEADS = 8
ATT_W = N_ATT_HEADS * HEAD_DIM
RWKV_W = N_RWKV_HEADS * HEAD_DIM
MIX_W = ATT_W + RWKV_W
BRANCHES = ((128, 1), (512, 4), (2048, 16))
MAX_WINDOW = 2048
Q_BLK = 128
ROPE_THETA = 10000.0
DECAY_LORA = 64
AAA_LORA = 64
MV_LORA = 32
GATE_LORA = 160
RWKV_COLS = 3 * RWKV_W + DECAY_LORA + AAA_LORA + GATE_LORA
IN_COLS = 3 * ATT_W + RWKV_COLS
D_FF = 2816
RMS_EPS = 1e-6
GN_EPS = 64e-5

kernel_name = 'hybrid_dilated_attn_rwkv7_decoder'

F32 = jnp.float32


def rmsnorm(x, g):
    xf = x.astype(F32)
    y = xf * lax.rsqrt(jnp.mean(xf * xf, -1, keepdims=True) + RMS_EPS)
    return (y * g.astype(F32)).astype(x.dtype)


def swiglu(x, wg, wu, wd):
    return (jax.nn.silu(x @ wg) * (x @ wu)) @ wd


def rope(x, pos):
    half = HEAD_DIM // 2
    inv = ROPE_THETA ** (-jnp.arange(half, dtype=F32) / half)
    ang = pos.astype(F32)[:, None] * inv[None, :]
    cos = jnp.cos(ang)[None, :, None, :]
    sin = jnp.sin(ang)[None, :, None, :]
    x1 = x[..., :half].astype(F32)
    x2 = x[..., half:].astype(F32)
    return jnp.concatenate([x1 * cos - x2 * sin, x2 * cos + x1 * sin], -1).astype(x.dtype)


def dilated_branch_prompt(q, k, v, dil, n_steps):
    B, S, H, E = q.shape
    L = S // dil
    nb = -(-L // Q_BLK)
    Lp = nb * Q_BLK

    def to_blocks(t):
        t = t.astype(F32).reshape(B, L, dil, H, E)
        t = jnp.pad(t, ((0, 0), (0, Lp - L), (0, 0), (0, 0), (0, 0)))
        return t.reshape(B, nb, Q_BLK, dil, H, E)

    def with_prev(t):
        prev = jnp.pad(t[:, :-1], ((0, 0), (1, 0), (0, 0), (0, 0), (0, 0), (0, 0)))
        return jnp.concatenate([prev, t], axis=2)

    qb = to_blocks(q)
    kc = with_prev(to_blocks(k))
    vc = with_prev(to_blocks(v))
    s = jnp.einsum('bnqrhe,bnkrhe->bnrhqk', qb, kc) * (E ** -0.5)
    qi = jnp.arange(Q_BLK)[:, None]
    ki = jnp.arange(2 * Q_BLK)[None, :] - Q_BLK
    dist = qi - ki
    start = (jnp.arange(nb) * Q_BLK)[:, None, None]
    valid = (dist >= 0) & (dist <= n_steps) & (start + ki >= 0)
    s = jnp.where(valid[None, :, None, None], s, -jnp.inf)
    m = jnp.max(s, -1)
    p = jnp.exp(s - m[..., None])
    den = jnp.sum(p, -1)
    o = jnp.einsum('bnrhqk,bnkrhe->bnqrhe', p, vc)
    m = m.transpose(0, 1, 4, 2, 3)
    den = den.transpose(0, 1, 4, 2, 3)
    o = o / den[..., None]

    def unblock(t):
        t = t.reshape((B, Lp, dil, H) + t.shape[5:])[:, :L]
        return t.reshape((B, S, H) + t.shape[4:])

    return unblock(o), unblock(m), unblock(den)


def dilated_branch_sample(q, kc, vc, buf_len, dil, n_steps):
    T, E = q.shape[1], q.shape[3]
    idx = buf_len + jnp.arange(T)[:, None] - dil * jnp.arange(n_steps + 1)[None, :]
    valid = idx >= 0
    idx = jnp.maximum(idx, 0)
    kg = kc[:, idx].astype(F32)
    vg = vc[:, idx].astype(F32)
    s = jnp.einsum('bthe,btshe->bhts', q.astype(F32), kg) * (E ** -0.5)
    s = jnp.where(valid[None, None], s, -jnp.inf)
    m = jnp.max(s, -1)
    p = jnp.exp(s - m[..., None])
    den = jnp.sum(p, -1)
    o = jnp.einsum('bhts,btshe->bthe', p, vg) / den.transpose(0, 2, 1)[..., None]
    return o, m.transpose(0, 2, 1), den.transpose(0, 2, 1)


def merge_branches(outs):
    M = jnp.max(jnp.stack([m for _, m, _ in outs]), 0)
    wts = [den * jnp.exp(m - M) for _, m, den in outs]
    tot = sum(wts)
    return sum(w[..., None] * o for (o, _, _), w in zip(outs, wts)) / tot[..., None]


def wkv_scan(r, w, k, v, kk, a, S0):
    xs = tuple(t.astype(F32).transpose(1, 0, 2, 3) for t in (r, w, k, v, kk, a))

    def step(S, inp):
        r_t, w_t, k_t, v_t, kk_t, a_t = inp
        sa = jnp.einsum('bhij,bhj->bhi', S, -kk_t)
        S = (S * w_t[:, :, None, :] + sa[..., None] * (kk_t * a_t)[:, :, None, :]
             + v_t[..., None] * k_t[:, :, None, :])
        return S, jnp.einsum('bhij,bhj->bhi', S, r_t)

    S_T, ys = lax.scan(step, S0.astype(F32), xs)
    return ys.transpose(1, 0, 2, 3), S_T


def rwkv_time_mix(z, shift_prev, wkv0, v_first, vres, mu, w0, decay_b, a0, a_b, g_b,
                  k_k, k_a, r_k, gn_w, gn_b):
    B, T, _ = z.shape
    z_prev = jnp.concatenate([shift_prev[:, None].astype(z.dtype), z[:, :-1]], axis=1)
    zm = z + (z_prev - z) * mu
    s1, s2, s3 = RWKV_W, 2 * RWKV_W, 3 * RWKV_W
    s4 = s3 + DECAY_LORA
    s5 = s4 + AAA_LORA
    r, k, v = zm[..., :s1], zm[..., s1:s2], zm[..., s2:s3]
    zw, za, zg = zm[..., s3:s4], zm[..., s4:s5], zm[..., s5:]
    w_log = -jax.nn.softplus(-(w0 + jnp.tanh(zw) @ decay_b)) - 0.5
    decay = jnp.exp(-jnp.exp(w_log.astype(F32)))
    a = jax.nn.sigmoid(a0 + za @ a_b)
    g = jax.nn.sigmoid(zg) @ g_b
    if vres is None:
        v_first = v
    else:
        v0, v_a, v_b = vres
        v = v + (v_first - v) * jax.nn.sigmoid(v0 + (v @ v_a) @ v_b)

    def heads(t):
        return t.reshape(B, T, N_RWKV_HEADS, HEAD_DIM)

    r_h, k_h, v_h, a_h, d_h = heads(r), heads(k), heads(v), heads(a), heads(decay)
    kk = heads(k * k_k).astype(F32)
    kk = kk / jnp.maximum(jnp.sqrt(jnp.sum(kk * kk, -1, keepdims=True)), 1e-12)
    k_h = k_h * (1 + (a_h - 1) * k_a.reshape(N_RWKV_HEADS, HEAD_DIM))
    y, wkv_T = wkv_scan(r_h, d_h, k_h, v_h, kk, a_h, wkv0)
    mean = jnp.mean(y, -1, keepdims=True)
    var = jnp.mean((y - mean) ** 2, -1, keepdims=True)
    y = (y - mean) * lax.rsqrt(var + GN_EPS) * gn_w.astype(F32) + gn_b.astype(F32)
    y = y + jnp.sum((r_h * k_h * r_k).astype(F32), -1, keepdims=True) * v_h.astype(F32)
    y = y.reshape(B, T, RWKV_W).astype(z.dtype) * g
    return y, v_first, wkv_T, z[:, -1]


def setup_inputs(seed: int = 0) -> dict:
    key = jax.random.key(seed)
    ks = iter(jax.random.split(key, 48))

    def nrm(shape, scale):
        return jax.random.normal(next(ks), shape, F32) * scale

    def uni(shape, lo, hi):
        return jax.random.uniform(next(ks), shape, F32, lo, hi)

    win = min(MAX_WINDOW, PAST_LEN)
    D, Dm1 = DEPTH, DEPTH - 1
    return {
        'x_prompt': nrm((BATCH, SEQ, D_MODEL), 1.0),
        'x_sample': nrm((DEC_BATCH, DEC_SEQ, D_MODEL), 1.0),
        'cache_k': nrm((D, DEC_BATCH, win, N_ATT_HEADS, HEAD_DIM), 1.0),
        'cache_v': nrm((D, DEC_BATCH, win, N_ATT_HEADS, HEAD_DIM), 1.0),
        'state_wkv': nrm((D, DEC_BATCH, N_RWKV_HEADS, HEAD_DIM, HEAD_DIM), 0.5),
        'state_shift': nrm((D, DEC_BATCH, RWKV_COLS), 1.0),
        'ffn1_norm': 1.0 + nrm((D, D_MODEL), 0.05),
        'ffn1_gate': nrm((D, D_MODEL, D_FF), D_MODEL ** -0.5),
        'ffn1_up': nrm((D, D_MODEL, D_FF), D_MODEL ** -0.5),
        'ffn1_down': nrm((D, D_FF, D_MODEL), D_FF ** -0.5),
        'mix_norm': 1.0 + nrm((D, D_MODEL), 0.05),
        'w_in': nrm((D, D_MODEL, IN_COLS), D_MODEL ** -0.5),
        'rwkv_mu': uni((D, RWKV_COLS), 0.0, 1.0),
        'rwkv_w0': uni((D, RWKV_W), -2.0, 1.0),
        'rwkv_decay_b': nrm((D, DECAY_LORA, RWKV_W), 0.1 * DECAY_LORA ** -0.5),
        'rwkv_a0': nrm((D, RWKV_W), 0.1),
        'rwkv_a_b': nrm((D, AAA_LORA, RWKV_W), 0.5 * AAA_LORA ** -0.5),
        'rwkv_g_b': nrm((D, GATE_LORA, RWKV_W), GATE_LORA ** -0.5),
        'rwkv_k_k': 0.85 + nrm((D, RWKV_W), 0.05),
        'rwkv_k_a': 1.0 + nrm((D, RWKV_W), 0.05),
        'rwkv_r_k': nrm((D, N_RWKV_HEADS, HEAD_DIM), 0.1),
        'rwkv_gn_w': 1.0 + nrm((D, N_RWKV_HEADS, HEAD_DIM), 0.05),
        'rwkv_gn_b': nrm((D, N_RWKV_HEADS, HEAD_DIM), 0.01),
        'rwkv_v0': 1.0 + nrm((Dm1, RWKV_W), 0.1),
        'rwkv_v_a': nrm((Dm1, RWKV_W, MV_LORA), RWKV_W ** -0.5),
        'rwkv_v_b': nrm((Dm1, MV_LORA, RWKV_W), MV_LORA ** -0.5),
        'w_out': nrm((D, MIX_W, D_MODEL), MIX_W ** -0.5),
        'ffn2_norm': 1.0 + nrm((D, D_MODEL), 0.05),
        'ffn2_gate': nrm((D, D_MODEL, D_FF), D_MODEL ** -0.5),
        'ffn2_up': nrm((D, D_MODEL, D_FF), D_MODEL ** -0.5),
        'ffn2_down': nrm((D, D_FF, D_MODEL), D_FF ** -0.5),
        'final_norm': 1.0 + nrm((D_MODEL,), 0.05),
    }


def reference(x_prompt, x_sample, cache_k, cache_v, state_wkv, state_shift,
              ffn1_norm, ffn1_gate, ffn1_up, ffn1_down, mix_norm, w_in,
              rwkv_mu, rwkv_w0, rwkv_decay_b, rwkv_a0, rwkv_a_b, rwkv_g_b,
              rwkv_k_k, rwkv_k_a, rwkv_r_k, rwkv_gn_w, rwkv_gn_b,
              rwkv_v0, rwkv_v_a, rwkv_v_b, w_out,
              ffn2_norm, ffn2_gate, ffn2_up, ffn2_down, final_norm):

    def run(x, pos, attend, keep, shift0, wkv0):
        h = x
        v_first = None
        new_k, new_v, new_wkv, new_shift = [], [], [], []
        for l in range(DEPTH):
            h = h + 0.5 * swiglu(rmsnorm(h, ffn1_norm[l]), ffn1_gate[l], ffn1_up[l], ffn1_down[l])
            z = rmsnorm(h, mix_norm[l]) @ w_in[l]
            B, T, _ = z.shape
            q = rope(z[..., :ATT_W].reshape(B, T, N_ATT_HEADS, HEAD_DIM), pos)
            k = rope(z[..., ATT_W:2 * ATT_W].reshape(B, T, N_ATT_HEADS, HEAD_DIM), pos)
            v = z[..., 2 * ATT_W:3 * ATT_W].reshape(B, T, N_ATT_HEADS, HEAD_DIM)
            o_att = attend(l, q, k, v).astype(h.dtype).reshape(B, T, ATT_W)
            vres = None if l == 0 else (rwkv_v0[l - 1], rwkv_v_a[l - 1], rwkv_v_b[l - 1])
            o_rwkv, v_first, wkv_T, shift_T = rwkv_time_mix(
                z[..., 3 * ATT_W:], shift0[l], wkv0[l], v_first, vres,
                rwkv_mu[l], rwkv_w0[l], rwkv_decay_b[l], rwkv_a0[l], rwkv_a_b[l], rwkv_g_b[l],
                rwkv_k_k[l], rwkv_k_a[l], rwkv_r_k[l], rwkv_gn_w[l], rwkv_gn_b[l])
            h = h + jnp.concatenate([o_att, o_rwkv], -1) @ w_out[l]
            h = h + 0.5 * swiglu(rmsnorm(h, ffn2_norm[l]), ffn2_gate[l], ffn2_up[l], ffn2_down[l])
            new_k.append(k[:, T - keep:])
            new_v.append(v[:, T - keep:])
            new_wkv.append(wkv_T)
            new_shift.append(shift_T)
        return (rmsnorm(h, final_norm), jnp.stack(new_k), jnp.stack(new_v),
                jnp.stack(new_wkv), jnp.stack(new_shift))

    Bp, Sp, _ = x_prompt.shape

    def prompt_attend(l, q, k, v):
        return merge_branches([dilated_branch_prompt(q, k, v, d, w // d) for w, d in BRANCHES])

    y_prompt, k_p, v_p, wkv_p, shift_p = run(
        x_prompt, jnp.arange(Sp), prompt_attend, min(MAX_WINDOW, Sp),
        jnp.zeros((DEPTH, Bp, RWKV_COLS), x_prompt.dtype),
        jnp.zeros((DEPTH, Bp, N_RWKV_HEADS, HEAD_DIM, HEAD_DIM), F32))

    Ts = x_sample.shape[1]
    buf_len = cache_k.shape[2]

    def sample_attend(l, q, k, v):
        kc = jnp.concatenate([cache_k[l].astype(k.dtype), k], axis=1)
        vc = jnp.concatenate([cache_v[l].astype(v.dtype), v], axis=1)
        return merge_branches([dilated_branch_sample(q, kc, vc, buf_len, d, w // d)
                               for w, d in BRANCHES])

    y_sample, k_s, v_s, wkv_s, shift_s = run(
        x_sample, PAST_LEN + jnp.arange(Ts), sample_attend, Ts, state_shift, state_wkv)

    return (y_prompt, y_sample, k_p, v_p, wkv_p, shift_p, k_s, v_s, wkv_s, shift_s)
```

```python
import functools

import jax
import jax.numpy as jnp
from jax import lax
from jax.experimental import pallas as pl
from jax.experimental.pallas import tpu as pltpu

F32 = jnp.float32
BF16 = jnp.bfloat16

HEAD_DIM = 64
N_ATT_HEADS = 8
N_RWKV_HEADS = 8
ATT_W = N_ATT_HEADS * HEAD_DIM
RWKV_W = N_RWKV_HEADS * HEAD_DIM
BRANCHES = ((128, 1), (512, 4), (2048, 16))
MAX_WINDOW = 2048
Q_BLK = 128
ROPE_THETA = 10000.0
DECAY_LORA = 64
AAA_LORA = 64
MV_LORA = 32
GATE_LORA = 160
LORA_W = DECAY_LORA + AAA_LORA + GATE_LORA
PAST_LEN = 8192
RMS_EPS = 1e-6
GN_EPS = 64e-5

LANES = 128
SUBLANES = 8
LORA_PAD = -(-LORA_W // LANES) * LANES
MV_PAD = LANES
HEADS_PER_VREG = LANES // HEAD_DIM
N_HEAD_PAIRS = RWKV_W // LANES
VMEM_LIMIT = 56 * 1024 * 1024
FF_CHUNK = 256
WKV_CHUNK = 64
ATT_TILE = MAX_WINDOW


def _dot(a, b):
    return jnp.dot(a, b, preferred_element_type=F32)


def _dot_nt(a, b):
    return lax.dot_general(a, b, (((1,), (1,)), ((), ())), preferred_element_type=F32)


def _bdot(a, b):
    return _dot(a.astype(BF16), b.astype(BF16))


def _split_dot(x, p):
    hi = x.astype(BF16)
    lo = (x - hi.astype(F32)).astype(BF16)
    return _dot(hi, p) + _dot(lo, p)


def _rms(x, g):
    return x * lax.rsqrt(jnp.mean(x * x, -1, keepdims=True) + RMS_EPS) * g


def _params(*sem):
    return pltpu.CompilerParams(dimension_semantics=sem, vmem_limit_bytes=VMEM_LIMIT)


def _const_spec(shape):
    nd = len(shape)
    return pl.BlockSpec(shape, lambda *_: (0,) * nd, pipeline_mode=pl.Buffered(1))


def _ffn_body(h_ref, g_ref, wg_ref, wu_ref, wd_ref, *rest, final):
    if final:
        fg_ref, o_ref, n_s, acc_s = rest
    else:
        o_ref, n_s, acc_s = rest
    x = h_ref[...]
    n_s[...] = _rms(x, g_ref[...]).astype(BF16)
    acc_s[...] = jnp.zeros_like(acc_s)

    for c in range(wg_ref.shape[1] // FF_CHUNK):
        cols = slice(c * FF_CHUNK, (c + 1) * FF_CHUNK)
        n = n_s[...]
        gt = _dot(n, wg_ref[:, cols])
        up = _dot(n, wu_ref[:, cols])
        act = (gt * jax.nn.sigmoid(gt) * up).astype(BF16)
        acc_s[...] += _dot(act, wd_ref[cols, :])
    y = x + 0.5 * acc_s[...]
    if final:
        y = _rms(y, fg_ref[...])
    o_ref[...] = y


def _ffn(h, g, wg3, wu3, wd3, tm, final_g=None):
    n, d = h.shape
    row = pl.BlockSpec((tm, d), lambda i: (i, 0))
    in_specs = [row, _const_spec((1, d)), _const_spec(wg3.shape), _const_spec(wu3.shape),
                _const_spec(wd3.shape)]
    args = [h, g, wg3, wu3, wd3]
    if final_g is not None:
        in_specs.append(_const_spec((1, d)))
        args.append(final_g)
    return pl.pallas_call(
        functools.partial(_ffn_body, final=final_g is not None),
        grid=(n // tm,),
        in_specs=in_specs,
        out_specs=row,
        out_shape=jax.ShapeDtypeStruct((n, d), F32),
        scratch_shapes=[pltpu.VMEM((tm, d), BF16), pltpu.VMEM((tm, d), F32)],
        compiler_params=_params("parallel"),
    )(*args)


def _rope(x, c, s1, s2):
    reps = x.shape[1] // LANES
    c, s1, s2 = (jnp.concatenate([t] * reps, axis=1) for t in (c, s1, s2))
    half = HEAD_DIM // 2
    upper = pltpu.roll(x, x.shape[1] - half, axis=1)
    lower = pltpu.roll(x, half, axis=1)
    return x * c + upper * s1 + lower * s2


def _inproj_body(h_ref, g_ref, wqkv_ref, wr_ref, wl_ref, c_ref, s1_ref, s2_ref,
                 q_ref, k_ref, v_ref, zr_ref, zl_ref):
    n = _rms(h_ref[...], g_ref[...]).astype(BF16)
    c, s1, s2 = c_ref[...], s1_ref[...], s2_ref[...]
    q = _dot(n, wqkv_ref[:, 0:ATT_W])
    q_ref[...] = _rope(q, c, s1, s2) * (HEAD_DIM ** -0.5)
    k = _dot(n, wqkv_ref[:, ATT_W:2 * ATT_W])
    k_ref[...] = _rope(k, c, s1, s2)
    v_ref[...] = _dot(n, wqkv_ref[:, 2 * ATT_W:3 * ATT_W])
    zr_ref[...] = _dot(n, wr_ref[...])
    zl_ref[...] = _dot(n, wl_ref[...])


def _inproj(h, g, wqkv, wr, wl, tabs, tm):
    n, d = h.shape
    tab_blocks = tabs[0].shape[0] // tm
    row = lambda w: pl.BlockSpec((tm, w), lambda i: (i, 0))
    tab = pl.BlockSpec((tm, LANES), lambda i: (i % tab_blocks, 0))
    widths = (ATT_W, ATT_W, ATT_W, 3 * RWKV_W, LORA_PAD)
    return pl.pallas_call(
        _inproj_body,
        grid=(n // tm,),
        in_specs=[row(d), _const_spec((1, d)), _const_spec(wqkv.shape), _const_spec(wr.shape),
                  _const_spec(wl.shape), tab, tab, tab],
        out_specs=[row(w) for w in widths],
        out_shape=[jax.ShapeDtypeStruct((n, w), F32) for w in widths],
        compiler_params=_params("parallel"),
    )(h, g, wqkv, wr, wl, *tabs)


def _attn_body(q_ref, kp_ref, kc_ref, vp_ref, vc_ref, o_ref, ks, vs, acc_s, m_s, den_s, *, tq):
    t = pl.program_id(2)
    ks[0:tq, :] = kp_ref[...]
    ks[tq:2 * tq, :] = kc_ref[...]
    vs[0:tq, :] = vp_ref[...]
    vs[tq:2 * tq, :] = vc_ref[...]
    lane = lax.broadcasted_iota(jnp.int32, (1, LANES), 1)
    head_a = lane < HEAD_DIM
    row = lax.broadcasted_iota(jnp.int32, (Q_BLK, 2 * Q_BLK), 0)
    col = lax.broadcasted_iota(jnp.int32, (Q_BLK, 2 * Q_BLK), 1)

    for bi, (window, dil) in enumerate(BRANCHES):
        n_steps = window // dil
        band = (col >= row + (Q_BLK - n_steps)) & (col <= row + Q_BLK)
        span = Q_BLK * dil

        def body(idx, carry, bi=bi, dil=dil, band=band, span=span):
            sb = idx // dil
            res = idx - sb * dil
            qbase = sb * span + res
            if dil == 1:
                qsl = pl.ds(pl.multiple_of(qbase, Q_BLK), Q_BLK)
                ksl = pl.ds(pl.multiple_of(tq + qbase - span, Q_BLK), 2 * Q_BLK)
            else:
                qsl = pl.ds(qbase, Q_BLK, stride=dil)
                ksl = pl.ds(tq + qbase - span, 2 * Q_BLK, stride=dil)
            q = q_ref[qsl, :]
            kb = ks[ksl, :].astype(BF16)
            vb = vs[ksl, :].astype(BF16)
            cmin = jnp.where(jnp.logical_and(t == 0, sb == 0), Q_BLK, 0)
            mask = band & (col >= cmin)
            parts = []
            for hm in (head_a, jnp.logical_not(head_a)):
                qh = jnp.where(hm, q, 0.0).astype(BF16)
                s = jnp.where(mask, _dot_nt(qh, kb), -jnp.inf)
                m = jnp.max(s, -1, keepdims=True)
                p = jnp.exp(s - m)
                den = jnp.sum(p, -1, keepdims=True)
                parts.append((_dot(p.astype(BF16), vb), m, den))
            (pa, ma, da), (pb, mb, db) = parts
            acc_s[bi, qsl, :] = jnp.where(head_a, pa, pb)
            m_s[bi, qsl, :] = jnp.where(head_a, ma, mb)
            den_s[bi, qsl, :] = jnp.where(head_a, da, db)
            return carry

        lax.fori_loop(0, tq // Q_BLK, body, 0)

    m_all = [m_s[i] for i in range(len(BRANCHES))]
    m_max = functools.reduce(jnp.maximum, m_all)
    wts = [jnp.exp(m - m_max) for m in m_all]
    num = sum(w * acc_s[i] for i, w in enumerate(wts))
    tot = sum(w * den_s[i] for i, w in enumerate(wts))
    o_ref[...] = num / tot


def _attn_prompt(q, k, v, batch, seq):
    n, w = q.shape
    tq = ATT_TILE
    assert seq % tq == 0
    nt = seq // tq
    cur = pl.BlockSpec((tq, LANES), lambda b, hp, t: (b * nt + t, hp))
    prev = pl.BlockSpec((tq, LANES), lambda b, hp, t: (b * nt + jnp.maximum(t - 1, 0), hp))
    nb = len(BRANCHES)
    return pl.pallas_call(
        functools.partial(_attn_body, tq=tq),
        grid=(batch, w // LANES, nt),
        in_specs=[cur, prev, cur, prev, cur],
        out_specs=cur,
        out_shape=jax.ShapeDtypeStruct((n, w), F32),
        scratch_shapes=[pltpu.VMEM((2 * tq, LANES), F32), pltpu.VMEM((2 * tq, LANES), F32),
                        pltpu.VMEM((nb, tq, LANES), F32), pltpu.VMEM((nb, tq, LANES), F32),
                        pltpu.VMEM((nb, tq, LANES), F32)],
        compiler_params=_params("parallel", "parallel", "arbitrary"),
    )(q, k, k, v, v)


def _attn_sample_body(q_ref, kn_ref, vn_ref, *rest):
    nb = len(BRANCHES)
    cache_refs, o_ref = rest[:2 * nb], rest[2 * nb]
    q, kn, vn = q_ref[0], kn_ref[0], vn_ref[0]
    s_new = jnp.sum(q * kn, -1, keepdims=True)
    parts = []
    for bi in range(nb):
        kc = cache_refs[2 * bi][0]
        vc = cache_refs[2 * bi + 1][0]
        s = jnp.sum(q[None] * kc, -1, keepdims=True)
        m = jnp.maximum(jnp.max(s, 0), s_new)
        p = jnp.exp(s - m[None])
        p_new = jnp.exp(s_new - m)
        den = jnp.sum(p, 0) + p_new
        acc = jnp.sum(p * vc, 0) + p_new * vn
        parts.append((acc, m, den))
    m_max = functools.reduce(jnp.maximum, [m for _, m, _ in parts])
    wts = [jnp.exp(m - m_max) for _, m, _ in parts]
    num = sum(w * acc for w, (acc, _, _) in zip(wts, parts))
    tot = sum(w * den for w, (_, _, den) in zip(wts, parts))
    o_ref[0] = num / tot


def _attn_sample(l, q, kn, vn, cache_k, cache_v):
    depth, b, buf_len, h, e = cache_k.shape
    cache_args, cache_specs = [], []
    for window, dil in BRANCHES:
        n_steps = window // dil
        assert buf_len % window == 0
        last = buf_len // window - 1
        spec = pl.BlockSpec((None, 1, n_steps, None, h, e),
                            lambda i, last=last: (l, i, last, 0, 0, 0))
        for c in (cache_k, cache_v):
            cache_args.append(c.reshape(depth, b, buf_len // dil, dil, h, e))
            cache_specs.append(spec)
    row = pl.BlockSpec((1, h, e), lambda i: (i, 0, 0))
    heads = lambda t: t.reshape(b, h, e)
    out = pl.pallas_call(
        _attn_sample_body,
        grid=(b,),
        in_specs=[row, row, row] + cache_specs,
        out_specs=row,
        out_shape=jax.ShapeDtypeStruct((b, h, e), F32),
        compiler_params=_params("parallel"),
    )(heads(q), heads(kn), heads(vn), *cache_args)
    return out.reshape(b, h * e)


def _prep_body(*refs, has_vres, rowwise_prev, tiles_per_seq):
    it = iter(refs)
    zr_ref, zl_ref = next(it), next(it)
    if rowwise_prev:
        zpr, zpl = next(it)[...], next(it)[...]
        zr, zl = zr_ref[...], zl_ref[...]
    else:
        pr8_ref, pl8_ref, s0r_ref, s0l_ref = next(it), next(it), next(it), next(it)
        zr, zl = zr_ref[...], zl_ref[...]
        is_start = (pl.program_id(0) % tiles_per_seq) == 0
        row0 = lax.broadcasted_iota(jnp.int32, (zr.shape[0], 1), 0) == 0
        last = SUBLANES - 1
        prev_r = jnp.where(is_start, s0r_ref[0], pr8_ref[last:last + 1, :])
        prev_l = jnp.where(is_start, s0l_ref[0], pl8_ref[last:last + 1, :])
        zpr = jnp.where(row0, prev_r, pltpu.roll(zr, 1, axis=0))
        zpl = jnp.where(row0, prev_l, pltpu.roll(zl, 1, axis=0))
    mur, mul, w0, wdec, a0, wa, wg, k_k, k_a, pmat = (next(it) for _ in range(10))
    if has_vres:
        vf_ref, v0, va, vb = next(it), next(it), next(it), next(it)
    r_o, lw_o, kt_o, v_o, av_o, bv_o, g_o = (next(it) for _ in range(7))

    zmr = zr + (zpr - zr) * mur[...]
    zml = zl + (zpl - zl) * mul[...]
    r = zmr[:, 0:RWKV_W]
    k = zmr[:, RWKV_W:2 * RWKV_W]
    v = zmr[:, 2 * RWKV_W:3 * RWKV_W]
    y = w0[...] + _dot(jnp.tanh(zml).astype(BF16), wdec[...])
    w_log = -(jnp.maximum(-y, 0.0) + jnp.log1p(jnp.exp(-jnp.abs(y)))) - 0.5
    lw_o[...] = -jnp.exp(w_log)
    a = jax.nn.sigmoid(a0[...] + _dot(zml.astype(BF16), wa[...]))
    g_o[...] = _dot(jax.nn.sigmoid(zml).astype(BF16), wg[...])
    if has_vres:
        mix = jax.nn.sigmoid(v0[...] + _dot(_dot(v.astype(BF16), va[...]).astype(BF16), vb[...]))
        v = v + (vf_ref[...] - v) * mix
    kk = k * k_k[...]
    norm = jnp.sqrt(_split_dot(kk * kk, pmat[...]))
    kk = kk / jnp.maximum(norm, 1e-12)
    r_o[...] = r
    kt_o[...] = k * (1.0 + (a - 1.0) * k_a[...])
    v_o[...] = v
    av_o[...] = -kk
    bv_o[...] = kk * a


def _prep(zr, zl, lw, tm, *, prev_rows=None, shift0=None, seq=None, v_first=None):
    n = zr.shape[0]
    row = lambda w: pl.BlockSpec((tm, w), lambda i: (i, 0))
    args, in_specs = [zr, zl], [row(3 * RWKV_W), row(LORA_PAD)]
    if prev_rows is not None:
        args += list(prev_rows)
        in_specs += [row(3 * RWKV_W), row(LORA_PAD)]
        tiles_per_seq = None
    else:
        tiles_per_seq = seq // tm
        t8 = tm // SUBLANES
        prev8 = lambda w: pl.BlockSpec((SUBLANES, w), lambda i: (jnp.maximum(i * t8 - 1, 0), 0))
        start = lambda w: pl.BlockSpec((1, 1, w), lambda i: (i // tiles_per_seq, 0, 0))
        args += [zr, zl, shift0[0][:, None], shift0[1][:, None]]
        in_specs += [prev8(3 * RWKV_W), prev8(LORA_PAD), start(3 * RWKV_W), start(LORA_PAD)]
    consts = [lw["mu_r"], lw["mu_l"], lw["w0"], lw["wdec"], lw["a0"], lw["wa"], lw["wg"],
              lw["k_k"], lw["k_a"], lw["pmat"]]
    args += consts
    in_specs += [_const_spec(c.shape) for c in consts]
    if v_first is not None:
        vres = [lw["v0"], lw["v_a"], lw["v_b"]]
        args += [v_first] + vres
        in_specs += [row(RWKV_W)] + [_const_spec(c.shape) for c in vres]
    return pl.pallas_call(
        functools.partial(_prep_body, has_vres=v_first is not None,
                          rowwise_prev=prev_rows is not None, tiles_per_seq=tiles_per_seq),
        grid=(n // tm,),
        in_specs=in_specs,
        out_specs=[row(RWKV_W)] * 7,
        out_shape=[jax.ShapeDtypeStruct((n, RWKV_W), F32)] * 7,
        compiler_params=_params("parallel"),
    )(*args)


def _cumsum_rows(x):
    rows = lax.broadcasted_iota(jnp.int32, (x.shape[0], 1), 0)
    s = 1
    while s < x.shape[0]:
        x = x + jnp.where(rows >= s, pltpu.roll(x, s, axis=0), 0.0)
        s *= 2
    return x


def _unit_lower_inverse(lmat, ri, ci):
    blk = lambda sh: (ri >> sh) == (ci >> sh)
    same16, same32 = blk(4), blk(5)
    eye = (ri == ci).astype(F32)
    n = lmat.shape[0]
    x = jnp.where(same16, lmat, 0.0)
    p = eye + x
    x = _bdot(x, x)
    for _ in range(2):
        xp = _bdot(x, jnp.concatenate([x, p], axis=1))
        x, p = xp[:, :n], p + xp[:, n:]
    t = p + _bdot(x, p)
    off = jnp.where(jnp.logical_and(same32, jnp.logical_not(same16)), lmat, 0.0)
    t = t + _bdot(t, _bdot(off, t))
    off = jnp.where(same32, 0.0, lmat)
    return t + _bdot(t, _bdot(off, t))


def _wkv_body(r_ref, lw_ref, kt_ref, v_ref, av_ref, bv_ref, s0_ref, y_ref, st_ref, st_s, *, chunk):
    t = pl.program_id(2)

    @pl.when(t == 0)
    def _():
        st_s[...] = s0_ref[0, 0]

    lane = lax.broadcasted_iota(jnp.int32, (1, LANES), 1)
    head_a = lane < HEAD_DIM
    n2 = HEADS_PER_VREG * chunk
    ri = lax.broadcasted_iota(jnp.int32, (n2, n2), 0)
    ci = lax.broadcasted_iota(jnp.int32, (n2, n2), 1)
    strict = ci < ri
    incl = ci <= ri

    def pstack(x):
        return jnp.concatenate([jnp.where(head_a, x, 0.0), jnp.where(head_a, 0.0, x)], axis=0)

    def body(c, carry):
        sl = pl.ds(pl.multiple_of(c * chunk, chunk), chunk)
        r, lw, kt, v, av, bv = (ref[sl, :] for ref in (r_ref, lw_ref, kt_ref, v_ref, av_ref, bv_ref))
        lc = _cumsum_rows(lw)
        lc_last = lc[chunk - 1:chunk, :]
        g_inv = jnp.exp(-lc)
        g_rem = jnp.exp(lc_last - lc)
        lhs = jnp.concatenate([pstack(av * jnp.exp(lc - lw)), pstack(r * jnp.exp(lc))], axis=0).astype(BF16)
        rhs = jnp.concatenate([pstack(bv * g_inv), pstack(kt * g_inv)], axis=0).astype(BF16)
        sc = _dot_nt(lhs, rhs)
        l_ab = jnp.where(strict, sc[0:n2, 0:n2], 0.0)
        l_ak = jnp.where(strict, sc[0:n2, n2:], 0.0)
        a_rb = jnp.where(incl, sc[n2:, 0:n2], 0.0)
        a_rk = jnp.where(incl, sc[n2:, n2:], 0.0)
        t_inv = _unit_lower_inverse(l_ab, ri, ci)

        state = st_s[...]
        hs = _dot_nt(lhs, state.astype(BF16))
        v2 = pstack(v)
        u2 = _bdot(t_inv, hs[0:n2] + _bdot(l_ak, v2))
        uv = jnp.concatenate([u2, v2], axis=0)
        y2 = hs[n2:] + _bdot(jnp.concatenate([a_rb, a_rk], axis=1), uv)
        y_ref[sl, :] = y2[0:chunk] + y2[chunk:]
        bk = jnp.concatenate([pstack(bv * g_rem), pstack(kt * g_rem)], axis=0)
        st_s[...] = state * jnp.exp(lc_last) + _bdot(uv.T, bk)
        return carry

    lax.fori_loop(0, r_ref.shape[0] // chunk, body, 0)

    @pl.when(t == pl.num_programs(2) - 1)
    def _():
        st_ref[0, 0] = st_s[...]


def _pair_state(s):
    b, h, n, _ = s.shape
    s = s.reshape(b, h // HEADS_PER_VREG, HEADS_PER_VREG, n, n)
    eye = jnp.eye(HEADS_PER_VREG, dtype=s.dtype)
    return jnp.einsum("bpaij,ac->bpaicj", s, eye).reshape(
        b, h // HEADS_PER_VREG, HEADS_PER_VREG * n, HEADS_PER_VREG * n)


def _unpair_state(s2):
    b, hp, n2, _ = s2.shape
    n = n2 // HEADS_PER_VREG
    s = s2.reshape(b, hp, HEADS_PER_VREG, n, HEADS_PER_VREG, n)
    s = jnp.stack([s[:, :, a, :, a, :] for a in range(HEADS_PER_VREG)], axis=2)
    return s.reshape(b, hp * HEADS_PER_VREG, n, n)


def _wkv_prompt(r, lw, kt, v, av, bv, s0, batch, seq, tb):
    n, w = r.shape
    nt = seq // tb
    n2 = HEADS_PER_VREG * HEAD_DIM
    tok = pl.BlockSpec((tb, LANES), lambda b, hp, t: (b * nt + t, hp))
    st = pl.BlockSpec((1, 1, n2, n2), lambda b, hp, t: (b, hp, 0, 0))
    y, s_t = pl.pallas_call(
        functools.partial(_wkv_body, chunk=WKV_CHUNK),
        grid=(batch, w // LANES, nt),
        in_specs=[tok] * 6 + [st],
        out_specs=[tok, st],
        out_shape=[jax.ShapeDtypeStruct((n, w), F32),
                   jax.ShapeDtypeStruct((batch, w // LANES, n2, n2), F32)],
        scratch_shapes=[pltpu.VMEM((n2, n2), F32)],
        compiler_params=_params("parallel", "parallel", "arbitrary"),
    )(r, lw, kt, v, av, bv, _pair_state(s0))
    return y, _unpair_state(s_t)


def _wkv_step_body(r_ref, lw_ref, kt_ref, av_ref, bv_ref, v_ref, s_ref, y_ref, so_ref):
    s = s_ref[...]
    sa = jnp.sum(s * av_ref[...], -1, keepdims=True)
    s = s * jnp.exp(lw_ref[...]) + sa * bv_ref[...] + v_ref[...] * kt_ref[...]
    so_ref[...] = s
    y_ref[...] = jnp.sum(s * r_ref[...], -1, keepdims=True)


def _wkv_step(r, lw, kt, v, av, bv, s0, bb):
    b, h, n, _ = s0.shape
    rowv = lambda x: x.reshape(b, h, 1, n)
    key_spec = pl.BlockSpec((bb, h, 1, n), lambda i: (i, 0, 0, 0))
    val_spec = pl.BlockSpec((bb, h, n, 1), lambda i: (i, 0, 0, 0))
    st_spec = pl.BlockSpec((bb, h, n, n), lambda i: (i, 0, 0, 0))
    y, s_t = pl.pallas_call(
        _wkv_step_body,
        grid=(b // bb,),
        in_specs=[key_spec] * 5 + [val_spec, st_spec],
        out_specs=[val_spec, st_spec],
        out_shape=[jax.ShapeDtypeStruct((b, h, n, 1), F32), jax.ShapeDtypeStruct(s0.shape, F32)],
        compiler_params=_params("parallel"),
    )(rowv(r), rowv(lw), rowv(kt), rowv(av), rowv(bv), v.reshape(b, h, n, 1), s0)
    return y.reshape(b, h * n), s_t


def _outproj_body(h_ref, oa_ref, y_ref, r_ref, kt_ref, v_ref, g_ref, rk_ref, gw_ref, gb_ref,
                  p_ref, woa_ref, wor_ref, o_ref):
    pmat = p_ref[...]
    y = y_ref[...]
    mean = _split_dot(y, pmat) * (1.0 / HEAD_DIM)
    d = y - mean
    var = _split_dot(d * d, pmat) * (1.0 / HEAD_DIM)
    yn = d * lax.rsqrt(var + GN_EPS) * gw_ref[...] + gb_ref[...]
    bonus = _split_dot(r_ref[...] * kt_ref[...] * rk_ref[...], pmat)
    yo = (yn + bonus * v_ref[...]) * g_ref[...]
    o_ref[...] = (h_ref[...] + _dot(oa_ref[...].astype(BF16), woa_ref[...])
                  + _dot(yo.astype(BF16), wor_ref[...]))


def _outproj(h, o_att, y, r, kt, v, g, lw, tm):
    n, d = h.shape
    row = lambda w: pl.BlockSpec((tm, w), lambda i: (i, 0))
    consts = [lw["r_k"], lw["gn_w"], lw["gn_b"], lw["pmat"], lw["wo_att"], lw["wo_rwkv"]]
    return pl.pallas_call(
        _outproj_body,
        grid=(n // tm,),
        in_specs=[row(d)] + [row(RWKV_W)] * 6 + [_const_spec(c.shape) for c in consts],
        out_specs=row(d),
        out_shape=jax.ShapeDtypeStruct((n, d), F32),
        compiler_params=_params("parallel"),
    )(h, o_att, y, r, kt, v, g, *consts)


def _rope_tables(pos):
    half = HEAD_DIM // 2
    inv = ROPE_THETA ** (-jnp.arange(half, dtype=F32) / half)
    ang = pos.astype(F32)[:, None] * inv[None, :]
    cos, sin = jnp.cos(ang), jnp.sin(ang)
    zero = jnp.zeros_like(sin)
    per_head = (jnp.concatenate([cos, cos], 1), jnp.concatenate([-sin, zero], 1),
                jnp.concatenate([zero, sin], 1))
    return tuple(jnp.concatenate([t] * HEADS_PER_VREG, 1) for t in per_head)


def _pad_rows(x, rows, at):
    return jnp.zeros((rows, x.shape[1]), x.dtype).at[at:at + x.shape[0]].set(x)


def _ffn_weights(wg, wu, wd):
    assert wg.shape[1] % FF_CHUNK == 0
    return wg.astype(BF16), wu.astype(BF16), wd.astype(BF16)


def _layer_weights(l, p):
    w_in = p["w_in"][l]
    rw0 = 3 * ATT_W
    lo0 = rw0 + 3 * RWKV_W
    row = lambda x: x.reshape(1, -1)
    head_ones = jnp.kron(jnp.eye(N_RWKV_HEADS, dtype=F32), jnp.ones((HEAD_DIM, HEAD_DIM), F32))
    lw = {
        "ffn1": _ffn_weights(p["ffn1_gate"][l], p["ffn1_up"][l], p["ffn1_down"][l]),
        "ffn2": _ffn_weights(p["ffn2_gate"][l], p["ffn2_up"][l], p["ffn2_down"][l]),
        "ffn1_norm": row(p["ffn1_norm"][l]),
        "ffn2_norm": row(p["ffn2_norm"][l]),
        "mix_norm": row(p["mix_norm"][l]),
        "wqkv": w_in[:, :rw0].astype(BF16),
        "wr": w_in[:, rw0:lo0].astype(BF16),
        "wl": jnp.pad(w_in[:, lo0:], ((0, 0), (0, LORA_PAD - LORA_W))).astype(BF16),
        "mu_r": row(p["rwkv_mu"][l][:3 * RWKV_W]),
        "mu_l": row(jnp.pad(p["rwkv_mu"][l][3 * RWKV_W:], (0, LORA_PAD - LORA_W))),
        "w0": row(p["rwkv_w0"][l]),
        "wdec": _pad_rows(p["rwkv_decay_b"][l], LORA_PAD, 0).astype(BF16),
        "a0": row(p["rwkv_a0"][l]),
        "wa": _pad_rows(p["rwkv_a_b"][l], LORA_PAD, DECAY_LORA).astype(BF16),
        "wg": _pad_rows(p["rwkv_g_b"][l], LORA_PAD, DECAY_LORA + AAA_LORA).astype(BF16),
        "k_k": row(p["rwkv_k_k"][l]),
        "k_a": row(p["rwkv_k_a"][l]),
        "r_k": row(p["rwkv_r_k"][l]),
        "gn_w": row(p["rwkv_gn_w"][l]),
        "gn_b": row(p["rwkv_gn_b"][l]),
        "pmat": head_ones.astype(BF16),
        "wo_att": p["w_out"][l][:ATT_W].astype(BF16),
        "wo_rwkv": p["w_out"][l][ATT_W:].astype(BF16),
    }
    if l > 0:
        lw["v0"] = row(p["rwkv_v0"][l - 1])
        lw["v_a"] = jnp.pad(p["rwkv_v_a"][l - 1], ((0, 0), (0, MV_PAD - MV_LORA))).astype(BF16)
        lw["v_b"] = _pad_rows(p["rwkv_v_b"][l - 1], MV_PAD, 0).astype(BF16)
    return lw


def _split_shift(shift):
    return shift[:, :3 * RWKV_W], jnp.pad(shift[:, 3 * RWKV_W:], ((0, 0), (0, LORA_PAD - LORA_W)))


def _run(x, layers, final_g, tabs, tm, mix):
    h = x
    v_first = None
    new_k, new_v, new_wkv, new_shift = [], [], [], []
    depth = len(layers)
    for l, lw in enumerate(layers):
        h = _ffn(h, lw["ffn1_norm"], *lw["ffn1"], tm)
        q, k, v, zr, zl = _inproj(h, lw["mix_norm"], lw["wqkv"], lw["wr"], lw["wl"], tabs, tm)
        o_att, (r, kt, vv, g, y), wkv_t, keep_k, keep_v, shift_t = mix(l, lw, q, k, v, zr, zl, v_first)
        if l == 0:
            v_first = vv
        h = _outproj(h, o_att, y, r, kt, vv, g, lw, tm)
        h = _ffn(h, lw["ffn2_norm"], *lw["ffn2"], tm, final_g if l == depth - 1 else None)
        new_k.append(keep_k)
        new_v.append(keep_v)
        new_wkv.append(wkv_t)
        new_shift.append(shift_t)
    return h, jnp.stack(new_k), jnp.stack(new_v), jnp.stack(new_wkv), jnp.stack(new_shift)


def kernel(x_prompt, x_sample, cache_k, cache_v, state_wkv, state_shift, ffn1_norm, ffn1_gate, ffn1_up, ffn1_down, mix_norm, w_in, rwkv_mu, rwkv_w0, rwkv_decay_b, rwkv_a0, rwkv_a_b, rwkv_g_b, rwkv_k_k, rwkv_k_a, rwkv_r_k, rwkv_gn_w, rwkv_gn_b, rwkv_v0, rwkv_v_a, rwkv_v_b, w_out, ffn2_norm, ffn2_gate, ffn2_up, ffn2_down, final_norm):
    p = dict(ffn1_norm=ffn1_norm, ffn1_gate=ffn1_gate, ffn1_up=ffn1_up, ffn1_down=ffn1_down,
             mix_norm=mix_norm, w_in=w_in, rwkv_mu=rwkv_mu, rwkv_w0=rwkv_w0,
             rwkv_decay_b=rwkv_decay_b, rwkv_a0=rwkv_a0, rwkv_a_b=rwkv_a_b, rwkv_g_b=rwkv_g_b,
             rwkv_k_k=rwkv_k_k, rwkv_k_a=rwkv_k_a, rwkv_r_k=rwkv_r_k, rwkv_gn_w=rwkv_gn_w,
             rwkv_gn_b=rwkv_gn_b, rwkv_v0=rwkv_v0, rwkv_v_a=rwkv_v_a, rwkv_v_b=rwkv_v_b,
             w_out=w_out, ffn2_norm=ffn2_norm, ffn2_gate=ffn2_gate, ffn2_up=ffn2_up,
             ffn2_down=ffn2_down)
    depth = w_in.shape[0]
    layers = [_layer_weights(l, p) for l in range(depth)]
    final_g = final_norm.reshape(1, -1)
    d_model = x_prompt.shape[-1]
    cols = 3 * RWKV_W + LORA_W

    def shift_out(zr, zl, rows):
        return jnp.concatenate([zr[rows], zl[rows, :LORA_W]], axis=-1)

    bp, sp, _ = x_prompt.shape
    keep = min(MAX_WINDOW, sp)
    tm_p = 512
    zero_shift = _split_shift(jnp.zeros((bp, cols), F32))
    zero_wkv = jnp.zeros((bp, N_RWKV_HEADS, HEAD_DIM, HEAD_DIM), F32)
    last_rows = jnp.arange(bp) * sp + sp - 1

    def heads(t, width):
        return t.reshape(bp, sp, width // HEAD_DIM, HEAD_DIM)[:, sp - keep:]

    def prompt_mix(l, lw, q, k, v, zr, zl, v_first):
        o_att = _attn_prompt(q, k, v, bp, sp)
        r, lwd, kt, vv, av, bv, g = _prep(zr, zl, lw, tm_p, shift0=zero_shift, seq=sp,
                                          v_first=v_first if l > 0 else None)
        y, wkv_t = _wkv_prompt(r, lwd, kt, vv, av, bv, zero_wkv, bp, sp, tm_p)
        return (o_att, (r, kt, vv, g, y), wkv_t, heads(k, ATT_W), heads(v, ATT_W),
                shift_out(zr, zl, last_rows))

    y_p, k_p, v_p, wkv_p, shift_p = _run(
        x_prompt.reshape(bp * sp, d_model), layers, final_g, _rope_tables(jnp.arange(sp)), tm_p,
        prompt_mix)

    bs, ts, _ = x_sample.shape
    assert ts == 1
    pos_s = jnp.tile(PAST_LEN + jnp.arange(ts), bs)

    def sample_mix(l, lw, q, k, v, zr, zl, v_first):
        o_att = _attn_sample(l, q, k, v, cache_k, cache_v)
        r, lwd, kt, vv, av, bv, g = _prep(zr, zl, lw, bs, prev_rows=_split_shift(state_shift[l]),
                                          v_first=v_first if l > 0 else None)
        y, wkv_t = _wkv_step(r, lwd, kt, vv, av, bv, state_wkv[l], SUBLANES)
        new_rows = lambda t: t.reshape(bs, ts, N_ATT_HEADS, HEAD_DIM)
        return (o_att, (r, kt, vv, g, y), wkv_t, new_rows(k), new_rows(v),
                shift_out(zr, zl, jnp.arange(bs)))

    y_s, k_s, v_s, wkv_s, shift_s = _run(
        x_sample.reshape(bs * ts, d_model), layers, final_g, _rope_tables(pos_s), bs * ts,
        sample_mix)

    return (y_p.reshape(bp, sp, d_model), y_s.reshape(bs, ts, d_model), k_p, v_p, wkv_p, shift_p,
            k_s, v_s, wkv_s, shift_s)
```

```python
import functools

import jax
import jax.numpy as jnp
from jax import lax
from jax.experimental import pallas as pl
from jax.experimental.pallas import tpu as pltpu

F32 = jnp.float32
BF16 = jnp.bfloat16

HEAD_DIM = 64
N_ATT_HEADS = 8
N_RWKV_HEADS = 8
ATT_W = N_ATT_HEADS * HEAD_DIM
RWKV_W = N_RWKV_HEADS * HEAD_DIM
BRANCHES = ((128, 1), (512, 4), (2048, 16))
MAX_WINDOW = 2048
Q_BLK = 128
ROPE_THETA = 10000.0
DECAY_LORA = 64
AAA_LORA = 64
MV_LORA = 32
GATE_LORA = 160
LORA_W = DECAY_LORA + AAA_LORA + GATE_LORA
PAST_LEN = 8192
RMS_EPS = 1e-6
GN_EPS = 64e-5

LANES = 128
SUBLANES = 8
LORA_PAD = -(-LORA_W // LANES) * LANES
MV_PAD = LANES
HEADS_PER_VREG = LANES // HEAD_DIM
N_HEAD_PAIRS = RWKV_W // LANES
VMEM_LIMIT = 56 * 1024 * 1024
FF_CHUNK = 256
WKV_CHUNK = 64
ATT_TILE = MAX_WINDOW
ATT_UNROLL = 4


def _dot(a, b):
    return jnp.dot(a, b, preferred_element_type=F32)


def _dot_nt(a, b):
    return lax.dot_general(a, b, (((1,), (1,)), ((), ())), preferred_element_type=F32)


def _bdot(a, b):
    return _dot(a.astype(BF16), b.astype(BF16))


def _split_dot(x, p):
    hi = x.astype(BF16)
    lo = (x - hi.astype(F32)).astype(BF16)
    return _dot(hi, p) + _dot(lo, p)


def _rms(x, g):
    return x * lax.rsqrt(jnp.mean(x * x, -1, keepdims=True) + RMS_EPS) * g


def _params(*sem):
    return pltpu.CompilerParams(dimension_semantics=sem, vmem_limit_bytes=VMEM_LIMIT)


def _const_spec(shape):
    nd = len(shape)
    return pl.BlockSpec(shape, lambda *_: (0,) * nd, pipeline_mode=pl.Buffered(1))


def _ffn_body(h_ref, g_ref, wg_ref, wu_ref, wd_ref, *rest, final):
    if final:
        fg_ref, o_ref, n_s, acc_s = rest
    else:
        o_ref, n_s, acc_s = rest
    x = h_ref[...]
    n_s[...] = _rms(x, g_ref[...]).astype(BF16)
    acc_s[...] = jnp.zeros_like(acc_s)

    for c in range(wg_ref.shape[1] // FF_CHUNK):
        cols = slice(c * FF_CHUNK, (c + 1) * FF_CHUNK)
        n = n_s[...]
        gt = _dot(n, wg_ref[:, cols])
        up = _dot(n, wu_ref[:, cols])
        act = (gt * jax.nn.sigmoid(gt) * up).astype(BF16)
        acc_s[...] += _dot(act, wd_ref[cols, :])
    y = x + 0.5 * acc_s[...]
    if final:
        y = _rms(y, fg_ref[...])
    o_ref[...] = y


def _ffn(h, g, wg3, wu3, wd3, tm, final_g=None):
    n, d = h.shape
    row = pl.BlockSpec((tm, d), lambda i: (i, 0))
    in_specs = [row, _const_spec((1, d)), _const_spec(wg3.shape), _const_spec(wu3.shape),
                _const_spec(wd3.shape)]
    args = [h, g, wg3, wu3, wd3]
    if final_g is not None:
        in_specs.append(_const_spec((1, d)))
        args.append(final_g)
    return pl.pallas_call(
        functools.partial(_ffn_body, final=final_g is not None),
        grid=(n // tm,),
        in_specs=in_specs,
        out_specs=row,
        out_shape=jax.ShapeDtypeStruct((n, d), F32),
        scratch_shapes=[pltpu.VMEM((tm, d), BF16), pltpu.VMEM((tm, d), F32)],
        compiler_params=_params("parallel"),
        name="ffn",
    )(*args)


def _rope(x, c, s1, s2):
    reps = x.shape[1] // LANES
    c, s1, s2 = (jnp.concatenate([t] * reps, axis=1) for t in (c, s1, s2))
    half = HEAD_DIM // 2
    upper = pltpu.roll(x, x.shape[1] - half, axis=1)
    lower = pltpu.roll(x, half, axis=1)
    return x * c + upper * s1 + lower * s2


def _inproj_body(h_ref, g_ref, wqkv_ref, wr_ref, wl_ref, c_ref, s1_ref, s2_ref,
                 q_ref, k_ref, v_ref, zr_ref, zl_ref):
    n = _rms(h_ref[...], g_ref[...]).astype(BF16)
    c, s1, s2 = c_ref[...], s1_ref[...], s2_ref[...]
    q = _dot(n, wqkv_ref[:, 0:ATT_W])
    q_ref[...] = _rope(q, c, s1, s2) * (HEAD_DIM ** -0.5)
    k = _dot(n, wqkv_ref[:, ATT_W:2 * ATT_W])
    k_ref[...] = _rope(k, c, s1, s2)
    v_ref[...] = _dot(n, wqkv_ref[:, 2 * ATT_W:3 * ATT_W])
    zr_ref[...] = _dot(n, wr_ref[...])
    zl_ref[...] = _dot(n, wl_ref[...])


def _inproj(h, g, wqkv, wr, wl, tabs, tm):
    n, d = h.shape
    tab_blocks = tabs[0].shape[0] // tm
    row = lambda w: pl.BlockSpec((tm, w), lambda i: (i, 0))
    tab = pl.BlockSpec((tm, LANES), lambda i: (i % tab_blocks, 0))
    widths = (ATT_W, ATT_W, ATT_W, 3 * RWKV_W, LORA_PAD)
    return pl.pallas_call(
        _inproj_body,
        grid=(n // tm,),
        in_specs=[row(d), _const_spec((1, d)), _const_spec(wqkv.shape), _const_spec(wr.shape),
                  _const_spec(wl.shape), tab, tab, tab],
        out_specs=[row(w) for w in widths],
        out_shape=[jax.ShapeDtypeStruct((n, w), F32) for w in widths],
        compiler_params=_params("parallel"),
        name="inproj_rope",
    )(h, g, wqkv, wr, wl, *tabs)


def _attn_body(q_ref, kp_ref, kc_ref, vp_ref, vc_ref, o_ref, ks, vs, acc_s, m_s, den_s, *, tq):
    t = pl.program_id(2)
    ks[0:tq, :] = kp_ref[...]
    ks[tq:2 * tq, :] = kc_ref[...]
    vs[0:tq, :] = vp_ref[...]
    vs[tq:2 * tq, :] = vc_ref[...]
    lane = lax.broadcasted_iota(jnp.int32, (1, LANES), 1)
    head_a = lane < HEAD_DIM
    row = lax.broadcasted_iota(jnp.int32, (Q_BLK, 2 * Q_BLK), 0)
    col = lax.broadcasted_iota(jnp.int32, (Q_BLK, 2 * Q_BLK), 1)

    for bi, (window, dil) in enumerate(BRANCHES):
        n_steps = window // dil
        band = (col >= row + (Q_BLK - n_steps)) & (col <= row + Q_BLK)
        span = Q_BLK * dil

        def block(idx, bi=bi, dil=dil, band=band, span=span):
            sb = idx // dil
            res = idx - sb * dil
            qbase = sb * span + res
            if dil == 1:
                qsl = pl.ds(pl.multiple_of(qbase, Q_BLK), Q_BLK)
                ksl = pl.ds(pl.multiple_of(tq + qbase - span, Q_BLK), 2 * Q_BLK)
            else:
                qsl = pl.ds(qbase, Q_BLK, stride=dil)
                ksl = pl.ds(tq + qbase - span, 2 * Q_BLK, stride=dil)
            q = q_ref[qsl, :]
            kb = ks[ksl, :].astype(BF16)
            vb = vs[ksl, :].astype(BF16)
            cmin = jnp.where(jnp.logical_and(t == 0, sb == 0), Q_BLK, 0)
            mask = band & (col >= cmin)
            scores = []
            for hm in (head_a, jnp.logical_not(head_a)):
                scores.append(_dot_nt(jnp.where(hm, q, 0.0).astype(BF16), kb))
                yield
            parts = []
            for s in scores:
                s = jnp.where(mask, s, -jnp.inf)
                m = jnp.max(s, -1, keepdims=True)
                p = jnp.exp(s - m)
                den = jnp.sum(p, -1, keepdims=True)
                parts.append((_dot(p.astype(BF16), vb), m, den))
                yield
            (pa, ma, da), (pb, mb, db) = parts
            acc_s[bi, qsl, :] = jnp.where(head_a, pa, pb)
            m_s[bi, qsl, :] = jnp.where(head_a, ma, mb)
            den_s[bi, qsl, :] = jnp.where(head_a, da, db)

        def body(i, carry, block=block):
            _lockstep([block(i * ATT_UNROLL + u) for u in range(ATT_UNROLL)])
            return carry

        lax.fori_loop(0, tq // Q_BLK // ATT_UNROLL, body, 0)

    m_all = [m_s[i] for i in range(len(BRANCHES))]
    m_max = functools.reduce(jnp.maximum, m_all)
    wts = [jnp.exp(m - m_max) for m in m_all]
    num = sum(w * acc_s[i] for i, w in enumerate(wts))
    tot = sum(w * den_s[i] for i, w in enumerate(wts))
    o_ref[...] = num / tot


def _attn_prompt(q, k, v, batch, seq):
    n, w = q.shape
    tq = ATT_TILE
    assert seq % tq == 0
    nt = seq // tq
    cur = pl.BlockSpec((tq, LANES), lambda b, hp, t: (b * nt + t, hp))
    prev = pl.BlockSpec((tq, LANES), lambda b, hp, t: (b * nt + jnp.maximum(t - 1, 0), hp))
    nb = len(BRANCHES)
    return pl.pallas_call(
        functools.partial(_attn_body, tq=tq),
        grid=(batch, w // LANES, nt),
        in_specs=[cur, prev, cur, prev, cur],
        out_specs=cur,
        out_shape=jax.ShapeDtypeStruct((n, w), F32),
        scratch_shapes=[pltpu.VMEM((2 * tq, LANES), F32), pltpu.VMEM((2 * tq, LANES), F32),
                        pltpu.VMEM((nb, tq, LANES), F32), pltpu.VMEM((nb, tq, LANES), F32),
                        pltpu.VMEM((nb, tq, LANES), F32)],
        compiler_params=_params("parallel", "parallel", "arbitrary"),
        name="attn_prompt",
    )(q, k, k, v, v)


def _attn_sample_body(q_ref, kn_ref, vn_ref, *rest):
    nb = len(BRANCHES)
    cache_refs, o_ref = rest[:2 * nb], rest[2 * nb]
    q, kn, vn = q_ref[0], kn_ref[0], vn_ref[0]
    s_new = jnp.sum(q * kn, -1, keepdims=True)
    parts = []
    for bi in range(nb):
        kc = cache_refs[2 * bi][0]
        vc = cache_refs[2 * bi + 1][0]
        s = jnp.sum(q[None] * kc, -1, keepdims=True)
        m = jnp.maximum(jnp.max(s, 0), s_new)
        p = jnp.exp(s - m[None])
        p_new = jnp.exp(s_new - m)
        den = jnp.sum(p, 0) + p_new
        acc = jnp.sum(p * vc, 0) + p_new * vn
        parts.append((acc, m, den))
    m_max = functools.reduce(jnp.maximum, [m for _, m, _ in parts])
    wts = [jnp.exp(m - m_max) for _, m, _ in parts]
    num = sum(w * acc for w, (acc, _, _) in zip(wts, parts))
    tot = sum(w * den for w, (_, _, den) in zip(wts, parts))
    o_ref[0] = num / tot


def _attn_sample(l, q, kn, vn, cache_k, cache_v):
    depth, b, buf_len, h, e = cache_k.shape
    cache_args, cache_specs = [], []
    for window, dil in BRANCHES:
        n_steps = window // dil
        assert buf_len % window == 0
        last = buf_len // window - 1
        spec = pl.BlockSpec((None, 1, n_steps, None, h, e),
                            lambda i, last=last: (l, i, last, 0, 0, 0))
        for c in (cache_k, cache_v):
            cache_args.append(c.reshape(depth, b, buf_len // dil, dil, h, e))
            cache_specs.append(spec)
    row = pl.BlockSpec((1, h, e), lambda i: (i, 0, 0))
    heads = lambda t: t.reshape(b, h, e)
    out = pl.pallas_call(
        _attn_sample_body,
        grid=(b,),
        in_specs=[row, row, row] + cache_specs,
        out_specs=row,
        out_shape=jax.ShapeDtypeStruct((b, h, e), F32),
        compiler_params=_params("parallel"),
        name="attn_sample",
    )(heads(q), heads(kn), heads(vn), *cache_args)
    return out.reshape(b, h * e)


def _prep_body(*refs, has_vres, rowwise_prev, tiles_per_seq):
    it = iter(refs)
    zr_ref, zl_ref = next(it), next(it)
    if rowwise_prev:
        zpr, zpl = next(it)[...], next(it)[...]
        zr, zl = zr_ref[...], zl_ref[...]
    else:
        pr8_ref, pl8_ref, s0r_ref, s0l_ref = next(it), next(it), next(it), next(it)
        zr, zl = zr_ref[...], zl_ref[...]
        is_start = (pl.program_id(0) % tiles_per_seq) == 0
        row0 = lax.broadcasted_iota(jnp.int32, (zr.shape[0], 1), 0) == 0
        last = SUBLANES - 1
        prev_r = jnp.where(is_start, s0r_ref[0], pr8_ref[last:last + 1, :])
        prev_l = jnp.where(is_start, s0l_ref[0], pl8_ref[last:last + 1, :])
        zpr = jnp.where(row0, prev_r, pltpu.roll(zr, 1, axis=0))
        zpl = jnp.where(row0, prev_l, pltpu.roll(zl, 1, axis=0))
    mur, mul, w0, wdec, a0, wa, wg, k_k, k_a, pmat = (next(it) for _ in range(10))
    if has_vres:
        vf_ref, v0, va, vb = next(it), next(it), next(it), next(it)
    r_o, lw_o, kt_o, v_o, av_o, bv_o, g_o = (next(it) for _ in range(7))

    zmr = zr + (zpr - zr) * mur[...]
    zml = zl + (zpl - zl) * mul[...]
    r = zmr[:, 0:RWKV_W]
    k = zmr[:, RWKV_W:2 * RWKV_W]
    v = zmr[:, 2 * RWKV_W:3 * RWKV_W]
    y = w0[...] + _dot(jnp.tanh(zml).astype(BF16), wdec[...])
    w_log = -(jnp.maximum(-y, 0.0) + jnp.log1p(jnp.exp(-jnp.abs(y)))) - 0.5
    lw_o[...] = -jnp.exp(w_log)
    a = jax.nn.sigmoid(a0[...] + _dot(zml.astype(BF16), wa[...]))
    g_o[...] = _dot(jax.nn.sigmoid(zml).astype(BF16), wg[...])
    if has_vres:
        mix = jax.nn.sigmoid(v0[...] + _dot(_dot(v.astype(BF16), va[...]).astype(BF16), vb[...]))
        v = v + (vf_ref[...] - v) * mix
    kk = k * k_k[...]
    norm = jnp.sqrt(_split_dot(kk * kk, pmat[...]))
    kk = kk / jnp.maximum(norm, 1e-12)
    r_o[...] = r
    kt_o[...] = k * (1.0 + (a - 1.0) * k_a[...])
    v_o[...] = v
    av_o[...] = -kk
    bv_o[...] = kk * a


def _prep(zr, zl, lw, tm, *, prev_rows=None, shift0=None, seq=None, v_first=None):
    n = zr.shape[0]
    row = lambda w: pl.BlockSpec((tm, w), lambda i: (i, 0))
    args, in_specs = [zr, zl], [row(3 * RWKV_W), row(LORA_PAD)]
    if prev_rows is not None:
        args += list(prev_rows)
        in_specs += [row(3 * RWKV_W), row(LORA_PAD)]
        tiles_per_seq = None
    else:
        tiles_per_seq = seq // tm
        t8 = tm // SUBLANES
        prev8 = lambda w: pl.BlockSpec((SUBLANES, w), lambda i: (jnp.maximum(i * t8 - 1, 0), 0))
        start = lambda w: pl.BlockSpec((1, 1, w), lambda i: (i // tiles_per_seq, 0, 0))
        args += [zr, zl, shift0[0][:, None], shift0[1][:, None]]
        in_specs += [prev8(3 * RWKV_W), prev8(LORA_PAD), start(3 * RWKV_W), start(LORA_PAD)]
    consts = [lw["mu_r"], lw["mu_l"], lw["w0"], lw["wdec"], lw["a0"], lw["wa"], lw["wg"],
              lw["k_k"], lw["k_a"], lw["pmat"]]
    args += consts
    in_specs += [_const_spec(c.shape) for c in consts]
    if v_first is not None:
        vres = [lw["v0"], lw["v_a"], lw["v_b"]]
        args += [v_first] + vres
        in_specs += [row(RWKV_W)] + [_const_spec(c.shape) for c in vres]
    return pl.pallas_call(
        functools.partial(_prep_body, has_vres=v_first is not None,
                          rowwise_prev=prev_rows is not None, tiles_per_seq=tiles_per_seq),
        grid=(n // tm,),
        in_specs=in_specs,
        out_specs=[row(RWKV_W)] * 7,
        out_shape=[jax.ShapeDtypeStruct((n, RWKV_W), F32)] * 7,
        compiler_params=_params("parallel"),
        name="rwkv_premix",
    )(*args)


def _cumsum_rows(x):
    rows = lax.broadcasted_iota(jnp.int32, (x.shape[0], 1), 0)
    s = 1
    while s < x.shape[0]:
        x = x + jnp.where(rows >= s, pltpu.roll(x, s, axis=0), 0.0)
        s *= 2
    return x


def _unit_lower_inverse(lmat, ri, ci):
    blk = lambda sh: (ri >> sh) == (ci >> sh)
    same16, same32 = blk(4), blk(5)
    eye = (ri == ci).astype(F32)
    n = lmat.shape[0]
    x = jnp.where(same16, lmat, 0.0)
    p = eye + x
    x = _bdot(x, x)
    yield
    for _ in range(2):
        xp = _bdot(x, jnp.concatenate([x, p], axis=1))
        yield
        x, p = xp[:, :n], p + xp[:, n:]
    t = p + _bdot(x, p)
    yield
    for off in (jnp.where(jnp.logical_and(same32, jnp.logical_not(same16)), lmat, 0.0),
                jnp.where(same32, 0.0, lmat)):
        ot = _bdot(off, t)
        yield
        t = t + _bdot(t, ot)
        yield
    return t


def _lockstep(gens):
    while gens:
        alive = []
        for g in gens:
            try:
                next(g)
                alive.append(g)
            except StopIteration:
                pass
        gens = alive


def _wkv_body(r_ref, lw_ref, kt_ref, v_ref, av_ref, bv_ref, s0_ref, y_ref, st_ref, st_s, *, chunk):
    t = pl.program_id(0)

    @pl.when(t == 0)
    def _():
        st_s[...] = s0_ref[...]

    lane = lax.broadcasted_iota(jnp.int32, (1, LANES), 1)
    head_a = lane < HEAD_DIM
    n2 = HEADS_PER_VREG * chunk
    ri = lax.broadcasted_iota(jnp.int32, (n2, n2), 0)
    ci = lax.broadcasted_iota(jnp.int32, (n2, n2), 1)
    strict = ci < ri
    incl = ci <= ri

    def pstack(x):
        return jnp.concatenate([jnp.where(head_a, x, 0.0), jnp.where(head_a, 0.0, x)], axis=0)

    def advance(b, hp, sl):
        cols = slice(hp * LANES, (hp + 1) * LANES)
        r, lw, kt, v, av, bv = (ref[b, sl, cols] for ref in (r_ref, lw_ref, kt_ref, v_ref, av_ref, bv_ref))
        lc = _cumsum_rows(lw)
        lc_last = lc[chunk - 1:chunk, :]
        g_inv = jnp.exp(-lc)
        g_rem = jnp.exp(lc_last - lc)
        lhs = jnp.concatenate([pstack(av * jnp.exp(lc - lw)), pstack(r * jnp.exp(lc))], axis=0).astype(BF16)
        rhs = jnp.concatenate([pstack(bv * g_inv), pstack(kt * g_inv)], axis=0).astype(BF16)
        sc = _dot_nt(lhs, rhs)
        yield
        l_ab = jnp.where(strict, sc[0:n2, 0:n2], 0.0)
        l_ak = jnp.where(strict, sc[0:n2, n2:], 0.0)
        a_rb = jnp.where(incl, sc[n2:, 0:n2], 0.0)
        a_rk = jnp.where(incl, sc[n2:, n2:], 0.0)
        v2 = pstack(v)
        akv = _bdot(l_ak, v2)
        yield
        t_inv = yield from _unit_lower_inverse(l_ab, ri, ci)

        state = st_s[b, hp]
        hs = _dot_nt(lhs, state.astype(BF16))
        yield
        u2 = _bdot(t_inv, hs[0:n2] + akv)
        yield
        uv = jnp.concatenate([u2, v2], axis=0)
        bk = jnp.concatenate([pstack(bv * g_rem), pstack(kt * g_rem)], axis=0)
        st_s[b, hp] = state * jnp.exp(lc_last) + _bdot(uv.T, bk)
        yield
        y2 = hs[n2:] + _bdot(jnp.concatenate([a_rb, a_rk], axis=1), uv)
        y_ref[b, sl, cols] = y2[0:chunk] + y2[chunk:]

    def body(c, carry):
        sl = pl.ds(pl.multiple_of(c * chunk, chunk), chunk)
        _lockstep([advance(b, hp, sl) for b in range(r_ref.shape[0])
                   for hp in range(r_ref.shape[2] // LANES)])
        return carry

    lax.fori_loop(0, r_ref.shape[1] // chunk, body, 0)

    @pl.when(t == pl.num_programs(0) - 1)
    def _():
        st_ref[...] = st_s[...]


def _pair_state(s):
    b, h, n, _ = s.shape
    s = s.reshape(b, h // HEADS_PER_VREG, HEADS_PER_VREG, n, n)
    eye = jnp.eye(HEADS_PER_VREG, dtype=s.dtype)
    return jnp.einsum("bpaij,ac->bpaicj", s, eye).reshape(
        b, h // HEADS_PER_VREG, HEADS_PER_VREG * n, HEADS_PER_VREG * n)


def _unpair_state(s2):
    b, hp, n2, _ = s2.shape
    n = n2 // HEADS_PER_VREG
    s = s2.reshape(b, hp, HEADS_PER_VREG, n, HEADS_PER_VREG, n)
    s = jnp.stack([s[:, :, a, :, a, :] for a in range(HEADS_PER_VREG)], axis=2)
    return s.reshape(b, hp * HEADS_PER_VREG, n, n)


def _wkv_prompt(r, lw, kt, v, av, bv, s0, batch, seq, tb):
    n, w = r.shape
    s2 = _pair_state(s0)
    tok = pl.BlockSpec((batch, tb, w), lambda t: (0, t, 0))
    st = pl.BlockSpec(s2.shape, lambda t: (0, 0, 0, 0))
    seqs = lambda x: x.reshape(batch, seq, w)
    y, s_t = pl.pallas_call(
        functools.partial(_wkv_body, chunk=WKV_CHUNK),
        grid=(seq // tb,),
        in_specs=[tok] * 6 + [st],
        out_specs=[tok, st],
        out_shape=[jax.ShapeDtypeStruct((batch, seq, w), F32), jax.ShapeDtypeStruct(s2.shape, F32)],
        scratch_shapes=[pltpu.VMEM(s2.shape, F32)],
        compiler_params=_params("arbitrary"),
        name="wkv_chunked",
    )(seqs(r), seqs(lw), seqs(kt), seqs(v), seqs(av), seqs(bv), s2)
    return y.reshape(n, w), _unpair_state(s_t)


def _wkv_step_body(r_ref, lw_ref, kt_ref, av_ref, bv_ref, v_ref, s_ref, y_ref, so_ref):
    s = s_ref[...]
    sa = jnp.sum(s * av_ref[...], -1, keepdims=True)
    s = s * jnp.exp(lw_ref[...]) + sa * bv_ref[...] + v_ref[...] * kt_ref[...]
    so_ref[...] = s
    y_ref[...] = jnp.sum(s * r_ref[...], -1, keepdims=True)


def _wkv_step(r, lw, kt, v, av, bv, s0, bb):
    b, h, n, _ = s0.shape
    rowv = lambda x: x.reshape(b, h, 1, n)
    key_spec = pl.BlockSpec((bb, h, 1, n), lambda i: (i, 0, 0, 0))
    val_spec = pl.BlockSpec((bb, h, n, 1), lambda i: (i, 0, 0, 0))
    st_spec = pl.BlockSpec((bb, h, n, n), lambda i: (i, 0, 0, 0))
    y, s_t = pl.pallas_call(
        _wkv_step_body,
        grid=(b // bb,),
        in_specs=[key_spec] * 5 + [val_spec, st_spec],
        out_specs=[val_spec, st_spec],
        out_shape=[jax.ShapeDtypeStruct((b, h, n, 1), F32), jax.ShapeDtypeStruct(s0.shape, F32)],
        compiler_params=_params("parallel"),
        name="wkv_step",
    )(rowv(r), rowv(lw), rowv(kt), rowv(av), rowv(bv), v.reshape(b, h, n, 1), s0)
    return y.reshape(b, h * n), s_t


def _outproj_body(h_ref, oa_ref, y_ref, r_ref, kt_ref, v_ref, g_ref, rk_ref, gw_ref, gb_ref,
                  p_ref, woa_ref, wor_ref, o_ref):
    pmat = p_ref[...]
    y = y_ref[...]
    mean = _split_dot(y, pmat) * (1.0 / HEAD_DIM)
    d = y - mean
    var = _split_dot(d * d, pmat) * (1.0 / HEAD_DIM)
    yn = d * lax.rsqrt(var + GN_EPS) * gw_ref[...] + gb_ref[...]
    bonus = _split_dot(r_ref[...] * kt_ref[...] * rk_ref[...], pmat)
    yo = (yn + bonus * v_ref[...]) * g_ref[...]
    o_ref[...] = (h_ref[...] + _dot(oa_ref[...].astype(BF16), woa_ref[...])
                  + _dot(yo.astype(BF16), wor_ref[...]))


def _outproj(h, o_att, y, r, kt, v, g, lw, tm):
    n, d = h.shape
    row = lambda w: pl.BlockSpec((tm, w), lambda i: (i, 0))
    consts = [lw["r_k"], lw["gn_w"], lw["gn_b"], lw["pmat"], lw["wo_att"], lw["wo_rwkv"]]
    return pl.pallas_call(
        _outproj_body,
        grid=(n // tm,),
        in_specs=[row(d)] + [row(RWKV_W)] * 6 + [_const_spec(c.shape) for c in consts],
        out_specs=row(d),
        out_shape=jax.ShapeDtypeStruct((n, d), F32),
        compiler_params=_params("parallel"),
        name="outproj",
    )(h, o_att, y, r, kt, v, g, *consts)


def _rope_tables(pos):
    half = HEAD_DIM // 2
    inv = ROPE_THETA ** (-jnp.arange(half, dtype=F32) / half)
    ang = pos.astype(F32)[:, None] * inv[None, :]
    cos, sin = jnp.cos(ang), jnp.sin(ang)
    zero = jnp.zeros_like(sin)
    per_head = (jnp.concatenate([cos, cos], 1), jnp.concatenate([-sin, zero], 1),
                jnp.concatenate([zero, sin], 1))
    return tuple(jnp.concatenate([t] * HEADS_PER_VREG, 1) for t in per_head)


def _pad_rows(x, rows, at):
    return jnp.zeros((rows, x.shape[1]), x.dtype).at[at:at + x.shape[0]].set(x)


def _ffn_weights(wg, wu, wd):
    assert wg.shape[1] % FF_CHUNK == 0
    return wg.astype(BF16), wu.astype(BF16), wd.astype(BF16)


def _layer_weights(l, p):
    w_in = p["w_in"][l]
    rw0 = 3 * ATT_W
    lo0 = rw0 + 3 * RWKV_W
    row = lambda x: x.reshape(1, -1)
    head_ones = jnp.kron(jnp.eye(N_RWKV_HEADS, dtype=F32), jnp.ones((HEAD_DIM, HEAD_DIM), F32))
    lw = {
        "ffn1": _ffn_weights(p["ffn1_gate"][l], p["ffn1_up"][l], p["ffn1_down"][l]),
        "ffn2": _ffn_weights(p["ffn2_gate"][l], p["ffn2_up"][l], p["ffn2_down"][l]),
        "ffn1_norm": row(p["ffn1_norm"][l]),
        "ffn2_norm": row(p["ffn2_norm"][l]),
        "mix_norm": row(p["mix_norm"][l]),
        "wqkv": w_in[:, :rw0].astype(BF16),
        "wr": w_in[:, rw0:lo0].astype(BF16),
        "wl": jnp.pad(w_in[:, lo0:], ((0, 0), (0, LORA_PAD - LORA_W))).astype(BF16),
        "mu_r": row(p["rwkv_mu"][l][:3 * RWKV_W]),
        "mu_l": row(jnp.pad(p["rwkv_mu"][l][3 * RWKV_W:], (0, LORA_PAD - LORA_W))),
        "w0": row(p["rwkv_w0"][l]),
        "wdec": _pad_rows(p["rwkv_decay_b"][l], LORA_PAD, 0).astype(BF16),
        "a0": row(p["rwkv_a0"][l]),
        "wa": _pad_rows(p["rwkv_a_b"][l], LORA_PAD, DECAY_LORA).astype(BF16),
        "wg": _pad_rows(p["rwkv_g_b"][l], LORA_PAD, DECAY_LORA + AAA_LORA).astype(BF16),
        "k_k": row(p["rwkv_k_k"][l]),
        "k_a": row(p["rwkv_k_a"][l]),
        "r_k": row(p["rwkv_r_k"][l]),
        "gn_w": row(p["rwkv_gn_w"][l]),
        "gn_b": row(p["rwkv_gn_b"][l]),
        "pmat": head_ones.astype(BF16),
        "wo_att": p["w_out"][l][:ATT_W].astype(BF16),
        "wo_rwkv": p["w_out"][l][ATT_W:].astype(BF16),
    }
    if l > 0:
        lw["v0"] = row(p["rwkv_v0"][l - 1])
        lw["v_a"] = jnp.pad(p["rwkv_v_a"][l - 1], ((0, 0), (0, MV_PAD - MV_LORA))).astype(BF16)
        lw["v_b"] = _pad_rows(p["rwkv_v_b"][l - 1], MV_PAD, 0).astype(BF16)
    return lw


def _split_shift(shift):
    return shift[:, :3 * RWKV_W], jnp.pad(shift[:, 3 * RWKV_W:], ((0, 0), (0, LORA_PAD - LORA_W)))


def _run(x, layers, final_g, tabs, tm, mix):
    h = x
    v_first = None
    new_k, new_v, new_wkv, new_shift = [], [], [], []
    depth = len(layers)
    for l, lw in enumerate(layers):
        h = _ffn(h, lw["ffn1_norm"], *lw["ffn1"], tm)
        q, k, v, zr, zl = _inproj(h, lw["mix_norm"], lw["wqkv"], lw["wr"], lw["wl"], tabs, tm)
        o_att, (r, kt, vv, g, y), wkv_t, keep_k, keep_v, shift_t = mix(l, lw, q, k, v, zr, zl, v_first)
        if l == 0:
            v_first = vv
        h = _outproj(h, o_att, y, r, kt, vv, g, lw, tm)
        h = _ffn(h, lw["ffn2_norm"], *lw["ffn2"], tm, final_g if l == depth - 1 else None)
        new_k.append(keep_k)
        new_v.append(keep_v)
        new_wkv.append(wkv_t)
        new_shift.append(shift_t)
    return h, jnp.stack(new_k), jnp.stack(new_v), jnp.stack(new_wkv), jnp.stack(new_shift)


def kernel(x_prompt, x_sample, cache_k, cache_v, state_wkv, state_shift, ffn1_norm, ffn1_gate, ffn1_up, ffn1_down, mix_norm, w_in, rwkv_mu, rwkv_w0, rwkv_decay_b, rwkv_a0, rwkv_a_b, rwkv_g_b, rwkv_k_k, rwkv_k_a, rwkv_r_k, rwkv_gn_w, rwkv_gn_b, rwkv_v0, rwkv_v_a, rwkv_v_b, w_out, ffn2_norm, ffn2_gate, ffn2_up, ffn2_down, final_norm):
    p = dict(ffn1_norm=ffn1_norm, ffn1_gate=ffn1_gate, ffn1_up=ffn1_up, ffn1_down=ffn1_down,
             mix_norm=mix_norm, w_in=w_in, rwkv_mu=rwkv_mu, rwkv_w0=rwkv_w0,
             rwkv_decay_b=rwkv_decay_b, rwkv_a0=rwkv_a0, rwkv_a_b=rwkv_a_b, rwkv_g_b=rwkv_g_b,
             rwkv_k_k=rwkv_k_k, rwkv_k_a=rwkv_k_a, rwkv_r_k=rwkv_r_k, rwkv_gn_w=rwkv_gn_w,
             rwkv_gn_b=rwkv_gn_b, rwkv_v0=rwkv_v0, rwkv_v_a=rwkv_v_a, rwkv_v_b=rwkv_v_b,
             w_out=w_out, ffn2_norm=ffn2_norm, ffn2_gate=ffn2_gate, ffn2_up=ffn2_up,
             ffn2_down=ffn2_down)
    depth = w_in.shape[0]
    layers = [_layer_weights(l, p) for l in range(depth)]
    final_g = final_norm.reshape(1, -1)
    d_model = x_prompt.shape[-1]
    cols = 3 * RWKV_W + LORA_W

    def shift_out(zr, zl, batch):
        last = lambda z: z.reshape(batch, -1, z.shape[-1])[:, -1]
        return jnp.concatenate([last(zr), last(zl)[:, :LORA_W]], axis=-1)

    bp, sp, _ = x_prompt.shape
    keep = min(MAX_WINDOW, sp)
    tm_p = 512
    zero_shift = _split_shift(jnp.zeros((bp, cols), F32))
    zero_wkv = jnp.zeros((bp, N_RWKV_HEADS, HEAD_DIM, HEAD_DIM), F32)

    def heads(t, width):
        window = t.reshape(bp, sp, width)[:, sp - keep:]
        return window.reshape(bp, keep, width // HEAD_DIM, HEAD_DIM)

    def prompt_mix(l, lw, q, k, v, zr, zl, v_first):
        o_att = _attn_prompt(q, k, v, bp, sp)
        r, lwd, kt, vv, av, bv, g = _prep(zr, zl, lw, tm_p, shift0=zero_shift, seq=sp,
                                          v_first=v_first if l > 0 else None)
        y, wkv_t = _wkv_prompt(r, lwd, kt, vv, av, bv, zero_wkv, bp, sp, tm_p)
        return (o_att, (r, kt, vv, g, y), wkv_t, heads(k, ATT_W), heads(v, ATT_W),
                shift_out(zr, zl, bp))

    y_p, k_p, v_p, wkv_p, shift_p = _run(
        x_prompt.reshape(bp * sp, d_model), layers, final_g, _rope_tables(jnp.arange(sp)), tm_p,
        prompt_mix)

    bs, ts, _ = x_sample.shape
    assert ts == 1
    pos_s = jnp.tile(PAST_LEN + jnp.arange(ts), bs)

    def sample_mix(l, lw, q, k, v, zr, zl, v_first):
        o_att = _attn_sample(l, q, k, v, cache_k, cache_v)
        r, lwd, kt, vv, av, bv, g = _prep(zr, zl, lw, bs, prev_rows=_split_shift(state_shift[l]),
                                          v_first=v_first if l > 0 else None)
        y, wkv_t = _wkv_step(r, lwd, kt, vv, av, bv, state_wkv[l], SUBLANES)
        new_rows = lambda t: t.reshape(bs, ts, N_ATT_HEADS, HEAD_DIM)
        return (o_att, (r, kt, vv, g, y), wkv_t, new_rows(k), new_rows(v),
                shift_out(zr, zl, bs))

    y_s, k_s, v_s, wkv_s, shift_s = _run(
        x_sample.reshape(bs * ts, d_model), layers, final_g, _rope_tables(pos_s), bs * ts,
        sample_mix)

    return (y_p.reshape(bp, sp, d_model), y_s.reshape(bs, ts, d_model), k_p, v_p, wkv_p, shift_p,
            k_s, v_s, wkv_s, shift_s)
```

```python
import functools

import jax
import jax.numpy as jnp
from jax import lax
from jax.experimental import pallas as pl
from jax.experimental.pallas import tpu as pltpu

F32 = jnp.float32
BF16 = jnp.bfloat16

HEAD_DIM = 64
N_ATT_HEADS = 8
N_RWKV_HEADS = 8
ATT_W = N_ATT_HEADS * HEAD_DIM
RWKV_W = N_RWKV_HEADS * HEAD_DIM
BRANCHES = ((128, 1), (512, 4), (2048, 16))
MAX_WINDOW = 2048
Q_BLK = 128
ROPE_THETA = 10000.0
DECAY_LORA = 64
AAA_LORA = 64
MV_LORA = 32
GATE_LORA = 160
LORA_W = DECAY_LORA + AAA_LORA + GATE_LORA
PAST_LEN = 8192
RMS_EPS = 1e-6
GN_EPS = 64e-5

LANES = 128
SUBLANES = 8
LORA_PAD = -(-LORA_W // LANES) * LANES
MV_PAD = LANES
HEADS_PER_VREG = LANES // HEAD_DIM
N_HEAD_PAIRS = RWKV_W // LANES
VMEM_LIMIT = 56 * 1024 * 1024
FF_CHUNK = 256
WKV_CHUNK = 64
ATT_TILE = MAX_WINDOW
ATT_UNROLL = 4


def _dot(a, b):
    return jnp.dot(a, b, preferred_element_type=F32)


def _dot_nt(a, b):
    return lax.dot_general(a, b, (((1,), (1,)), ((), ())), preferred_element_type=F32)


def _bdot(a, b):
    return _dot(a.astype(BF16), b.astype(BF16))


def _split_dot(x, p):
    hi = x.astype(BF16)
    lo = (x - hi.astype(F32)).astype(BF16)
    return _dot(hi, p) + _dot(lo, p)


def _rms(x, g):
    return x * lax.rsqrt(jnp.mean(x * x, -1, keepdims=True) + RMS_EPS) * g


def _params(*sem):
    return pltpu.CompilerParams(dimension_semantics=sem, vmem_limit_bytes=VMEM_LIMIT)


def _const_spec(shape):
    nd = len(shape)
    return pl.BlockSpec(shape, lambda *_: (0,) * nd, pipeline_mode=pl.Buffered(1))


def _ffn_body(h_ref, g_ref, wg_ref, wu_ref, wd_ref, *rest, final):
    if final:
        fg_ref, o_ref, n_s, acc_s = rest
    else:
        o_ref, n_s, acc_s = rest
    x = h_ref[...]
    n_s[...] = _rms(x, g_ref[...]).astype(BF16)
    acc_s[...] = jnp.zeros_like(acc_s)

    for c in range(wg_ref.shape[1] // FF_CHUNK):
        cols = slice(c * FF_CHUNK, (c + 1) * FF_CHUNK)
        n = n_s[...]
        gt = _dot(n, wg_ref[:, cols])
        up = _dot(n, wu_ref[:, cols])
        act = (gt * jax.nn.sigmoid(gt) * up).astype(BF16)
        acc_s[...] += _dot(act, wd_ref[cols, :])
    y = x + 0.5 * acc_s[...]
    if final:
        y = _rms(y, fg_ref[...])
    o_ref[...] = y


def _ffn(h, g, wg3, wu3, wd3, tm, final_g=None):
    n, d = h.shape
    row = pl.BlockSpec((tm, d), lambda i: (i, 0))
    in_specs = [row, _const_spec((1, d)), _const_spec(wg3.shape), _const_spec(wu3.shape),
                _const_spec(wd3.shape)]
    args = [h, g, wg3, wu3, wd3]
    if final_g is not None:
        in_specs.append(_const_spec((1, d)))
        args.append(final_g)
    return pl.pallas_call(
        functools.partial(_ffn_body, final=final_g is not None),
        grid=(n // tm,),
        in_specs=in_specs,
        out_specs=row,
        out_shape=jax.ShapeDtypeStruct((n, d), F32),
        scratch_shapes=[pltpu.VMEM((tm, d), BF16), pltpu.VMEM((tm, d), F32)],
        compiler_params=_params("parallel"),
        name="ffn",
    )(*args)


def _rope(x, c, s1, s2):
    reps = x.shape[1] // LANES
    c, s1, s2 = (jnp.concatenate([t] * reps, axis=1) for t in (c, s1, s2))
    half = HEAD_DIM // 2
    upper = pltpu.roll(x, x.shape[1] - half, axis=1)
    lower = pltpu.roll(x, half, axis=1)
    return x * c + upper * s1 + lower * s2


def _inproj_body(h_ref, g_ref, wqkv_ref, wr_ref, wl_ref, c_ref, s1_ref, s2_ref,
                 q_ref, k_ref, v_ref, zr_ref, zl_ref):
    n = _rms(h_ref[...], g_ref[...]).astype(BF16)
    c, s1, s2 = c_ref[...], s1_ref[...], s2_ref[...]
    q = _dot(n, wqkv_ref[:, 0:ATT_W])
    q_ref[...] = _rope(q, c, s1, s2) * (HEAD_DIM ** -0.5)
    k = _dot(n, wqkv_ref[:, ATT_W:2 * ATT_W])
    k_ref[...] = _rope(k, c, s1, s2)
    v_ref[...] = _dot(n, wqkv_ref[:, 2 * ATT_W:3 * ATT_W])
    zr_ref[...] = _dot(n, wr_ref[...])
    zl_ref[...] = _dot(n, wl_ref[...])


def _inproj(h, g, wqkv, wr, wl, tabs, tm):
    n, d = h.shape
    tab_blocks = tabs[0].shape[0] // tm
    row = lambda w: pl.BlockSpec((tm, w), lambda i: (i, 0))
    tab = pl.BlockSpec((tm, LANES), lambda i: (i % tab_blocks, 0))
    widths = (ATT_W, ATT_W, ATT_W, 3 * RWKV_W, LORA_PAD)
    return pl.pallas_call(
        _inproj_body,
        grid=(n // tm,),
        in_specs=[row(d), _const_spec((1, d)), _const_spec(wqkv.shape), _const_spec(wr.shape),
                  _const_spec(wl.shape), tab, tab, tab],
        out_specs=[row(w) for w in widths],
        out_shape=[jax.ShapeDtypeStruct((n, w), F32) for w in widths],
        compiler_params=_params("parallel"),
        name="inproj_rope",
    )(h, g, wqkv, wr, wl, *tabs)


def _attn_body(q_ref, kp_ref, kc_ref, vp_ref, vc_ref, o_ref, ks, vs, acc_s, m_s, den_s, *, tq):
    t = pl.program_id(2)
    ks[0:tq, :] = kp_ref[...]
    ks[tq:2 * tq, :] = kc_ref[...]
    vs[0:tq, :] = vp_ref[...]
    vs[tq:2 * tq, :] = vc_ref[...]
    lane = lax.broadcasted_iota(jnp.int32, (1, LANES), 1)
    head_a = lane < HEAD_DIM
    row = lax.broadcasted_iota(jnp.int32, (Q_BLK, 2 * Q_BLK), 0)
    col = lax.broadcasted_iota(jnp.int32, (Q_BLK, 2 * Q_BLK), 1)

    for bi, (window, dil) in enumerate(BRANCHES):
        n_steps = window // dil
        band = (col >= row + (Q_BLK - n_steps)) & (col <= row + Q_BLK)
        span = Q_BLK * dil

        def block(idx, bi=bi, dil=dil, band=band, span=span):
            sb = idx // dil
            res = idx - sb * dil
            qbase = sb * span + res
            if dil == 1:
                qsl = pl.ds(pl.multiple_of(qbase, Q_BLK), Q_BLK)
                ksl = pl.ds(pl.multiple_of(tq + qbase - span, Q_BLK), 2 * Q_BLK)
            else:
                qsl = pl.ds(qbase, Q_BLK, stride=dil)
                ksl = pl.ds(tq + qbase - span, 2 * Q_BLK, stride=dil)
            q = q_ref[qsl, :]
            kb = ks[ksl, :].astype(BF16)
            vb = vs[ksl, :].astype(BF16)
            cmin = jnp.where(jnp.logical_and(t == 0, sb == 0), Q_BLK, 0)
            mask = band & (col >= cmin)
            scores = []
            for hm in (head_a, jnp.logical_not(head_a)):
                scores.append(_dot_nt(jnp.where(hm, q, 0.0).astype(BF16), kb))
                yield
            parts = []
            for s in scores:
                s = jnp.where(mask, s, -jnp.inf)
                m = jnp.max(s, -1, keepdims=True)
                p = jnp.exp(s - m)
                den = jnp.sum(p, -1, keepdims=True)
                parts.append((_dot(p.astype(BF16), vb), m, den))
                yield
            (pa, ma, da), (pb, mb, db) = parts
            acc_s[bi, qsl, :] = jnp.where(head_a, pa, pb)
            m_s[bi, qsl, :] = jnp.where(head_a, ma, mb)
            den_s[bi, qsl, :] = jnp.where(head_a, da, db)

        def body(i, carry, block=block):
            _lockstep([block(i * ATT_UNROLL + u) for u in range(ATT_UNROLL)])
            return carry

        lax.fori_loop(0, tq // Q_BLK // ATT_UNROLL, body, 0)

    m_all = [m_s[i] for i in range(len(BRANCHES))]
    m_max = functools.reduce(jnp.maximum, m_all)
    wts = [jnp.exp(m - m_max) for m in m_all]
    num = sum(w * acc_s[i] for i, w in enumerate(wts))
    tot = sum(w * den_s[i] for i, w in enumerate(wts))
    o_ref[...] = num / tot


def _attn_prompt(q, k, v, batch, seq):
    n, w = q.shape
    tq = ATT_TILE
    assert seq % tq == 0
    nt = seq // tq
    cur = pl.BlockSpec((tq, LANES), lambda b, hp, t: (b * nt + t, hp))
    prev = pl.BlockSpec((tq, LANES), lambda b, hp, t: (b * nt + jnp.maximum(t - 1, 0), hp))
    nb = len(BRANCHES)
    return pl.pallas_call(
        functools.partial(_attn_body, tq=tq),
        grid=(batch, w // LANES, nt),
        in_specs=[cur, prev, cur, prev, cur],
        out_specs=cur,
        out_shape=jax.ShapeDtypeStruct((n, w), F32),
        scratch_shapes=[pltpu.VMEM((2 * tq, LANES), F32), pltpu.VMEM((2 * tq, LANES), F32),
                        pltpu.VMEM((nb, tq, LANES), F32), pltpu.VMEM((nb, tq, LANES), F32),
                        pltpu.VMEM((nb, tq, LANES), F32)],
        compiler_params=_params("parallel", "parallel", "arbitrary"),
        name="attn_prompt",
    )(q, k, k, v, v)


def _attn_sample_body(qt_ref, knt_ref, vnt_ref, kt_ref, vt_ref, o_ref):
    qt, knt, vnt = qt_ref[0], knt_ref[0], vnt_ref[0]
    n_heads, _, buf_len = kt_ref.shape
    pos = lax.broadcasted_iota(jnp.int32, (1, buf_len), 1)
    dist = buf_len - pos
    cnt = sum((jnp.logical_and(lax.rem(dist, dil) == 0, dist <= window)).astype(F32)
              for window, dil in BRANCHES)
    live = cnt > 0.0
    head_lane = lax.broadcasted_iota(jnp.int32, (1, n_heads), 1)
    s_new_all = jnp.sum(qt * knt, 0, keepdims=True)
    out = jnp.zeros_like(qt)
    for h in range(n_heads):
        pick = lambda x: jnp.sum(jnp.where(head_lane == h, x, 0.0), 1, keepdims=True)
        s = jnp.sum(pick(qt) * kt_ref[h], 0, keepdims=True)
        s_new = pick(s_new_all)
        m = jnp.maximum(jnp.max(jnp.where(live, s, -jnp.inf), -1, keepdims=True), s_new)
        c = jnp.where(live, cnt * jnp.exp(s - m), 0.0)
        c_new = len(BRANCHES) * jnp.exp(s_new - m)
        tot = jnp.sum(c, -1, keepdims=True) + c_new
        num = jnp.sum(c * vt_ref[h], -1, keepdims=True) + c_new * pick(vnt)
        out = jnp.where(head_lane == h, num / tot, out)
    o_ref[0] = out


def _attn_sample(l, q, kn, vn, cache_k, cache_v):
    depth, b, buf_len, h, e = cache_k.shape
    assert all(buf_len >= window for window, _ in BRANCHES)
    by_pos = lambda c: jnp.transpose(c, (0, 1, 3, 4, 2))
    cache_spec = pl.BlockSpec((None, None, h, e, buf_len), lambda i: (l, i, 0, 0, 0))
    row = pl.BlockSpec((1, e, h), lambda i: (i, 0, 0))
    cols = lambda t: t.reshape(b, h, e).transpose(0, 2, 1)
    out = pl.pallas_call(
        _attn_sample_body,
        grid=(b,),
        in_specs=[row, row, row, cache_spec, cache_spec],
        out_specs=row,
        out_shape=jax.ShapeDtypeStruct((b, e, h), F32),
        compiler_params=_params("parallel"),
        name="attn_sample",
    )(cols(q), cols(kn), cols(vn), by_pos(cache_k), by_pos(cache_v))
    return out.transpose(0, 2, 1).reshape(b, h * e)


def _prep_body(*refs, has_vres, rowwise_prev, tiles_per_seq):
    it = iter(refs)
    zr_ref, zl_ref = next(it), next(it)
    if rowwise_prev:
        zpr, zpl = next(it)[...], next(it)[...]
        zr, zl = zr_ref[...], zl_ref[...]
    else:
        pr8_ref, pl8_ref, s0r_ref, s0l_ref = next(it), next(it), next(it), next(it)
        zr, zl = zr_ref[...], zl_ref[...]
        is_start = (pl.program_id(0) % tiles_per_seq) == 0
        row0 = lax.broadcasted_iota(jnp.int32, (zr.shape[0], 1), 0) == 0
        last = SUBLANES - 1
        prev_r = jnp.where(is_start, s0r_ref[0], pr8_ref[last:last + 1, :])
        prev_l = jnp.where(is_start, s0l_ref[0], pl8_ref[last:last + 1, :])
        zpr = jnp.where(row0, prev_r, pltpu.roll(zr, 1, axis=0))
        zpl = jnp.where(row0, prev_l, pltpu.roll(zl, 1, axis=0))
    mur, mul, w0, wdec, a0, wa, wg, k_k, k_a, pmat = (next(it) for _ in range(10))
    if has_vres:
        vf_ref, v0, va, vb = next(it), next(it), next(it), next(it)
    r_o, lw_o, kt_o, v_o, av_o, bv_o, g_o = (next(it) for _ in range(7))

    zmr = zr + (zpr - zr) * mur[...]
    zml = zl + (zpl - zl) * mul[...]
    r = zmr[:, 0:RWKV_W]
    k = zmr[:, RWKV_W:2 * RWKV_W]
    v = zmr[:, 2 * RWKV_W:3 * RWKV_W]
    y = w0[...] + _dot(jnp.tanh(zml).astype(BF16), wdec[...])
    w_log = -(jnp.maximum(-y, 0.0) + jnp.log1p(jnp.exp(-jnp.abs(y)))) - 0.5
    lw_o[...] = -jnp.exp(w_log)
    a = jax.nn.sigmoid(a0[...] + _dot(zml.astype(BF16), wa[...]))
    g_o[...] = _dot(jax.nn.sigmoid(zml).astype(BF16), wg[...])
    if has_vres:
        mix = jax.nn.sigmoid(v0[...] + _dot(_dot(v.astype(BF16), va[...]).astype(BF16), vb[...]))
        v = v + (vf_ref[...] - v) * mix
    kk = k * k_k[...]
    norm = jnp.sqrt(_split_dot(kk * kk, pmat[...]))
    kk = kk / jnp.maximum(norm, 1e-12)
    r_o[...] = r
    kt_o[...] = k * (1.0 + (a - 1.0) * k_a[...])
    v_o[...] = v
    av_o[...] = -kk
    bv_o[...] = kk * a


def _prep(zr, zl, lw, tm, *, prev_rows=None, shift0=None, seq=None, v_first=None):
    n = zr.shape[0]
    row = lambda w: pl.BlockSpec((tm, w), lambda i: (i, 0))
    args, in_specs = [zr, zl], [row(3 * RWKV_W), row(LORA_PAD)]
    if prev_rows is not None:
        args += list(prev_rows)
        in_specs += [row(3 * RWKV_W), row(LORA_PAD)]
        tiles_per_seq = None
    else:
        tiles_per_seq = seq // tm
        t8 = tm // SUBLANES
        prev8 = lambda w: pl.BlockSpec((SUBLANES, w), lambda i: (jnp.maximum(i * t8 - 1, 0), 0))
        start = lambda w: pl.BlockSpec((1, 1, w), lambda i: (i // tiles_per_seq, 0, 0))
        args += [zr, zl, shift0[0][:, None], shift0[1][:, None]]
        in_specs += [prev8(3 * RWKV_W), prev8(LORA_PAD), start(3 * RWKV_W), start(LORA_PAD)]
    consts = [lw["mu_r"], lw["mu_l"], lw["w0"], lw["wdec"], lw["a0"], lw["wa"], lw["wg"],
              lw["k_k"], lw["k_a"], lw["pmat"]]
    args += consts
    in_specs += [_const_spec(c.shape) for c in consts]
    if v_first is not None:
        vres = [lw["v0"], lw["v_a"], lw["v_b"]]
        args += [v_first] + vres
        in_specs += [row(RWKV_W)] + [_const_spec(c.shape) for c in vres]
    return pl.pallas_call(
        functools.partial(_prep_body, has_vres=v_first is not None,
                          rowwise_prev=prev_rows is not None, tiles_per_seq=tiles_per_seq),
        grid=(n // tm,),
        in_specs=in_specs,
        out_specs=[row(RWKV_W)] * 7,
        out_shape=[jax.ShapeDtypeStruct((n, RWKV_W), F32)] * 7,
        compiler_params=_params("parallel"),
        name="rwkv_premix",
    )(*args)


def _cumsum_rows(x):
    rows = lax.broadcasted_iota(jnp.int32, (x.shape[0], 1), 0)
    s = 1
    while s < x.shape[0]:
        x = x + jnp.where(rows >= s, pltpu.roll(x, s, axis=0), 0.0)
        s *= 2
    return x


def _unit_lower_inverse(lmat, ri, ci):
    blk = lambda sh: (ri >> sh) == (ci >> sh)
    same16, same32 = blk(4), blk(5)
    eye = (ri == ci).astype(F32)
    n = lmat.shape[0]
    x = jnp.where(same16, lmat, 0.0)
    p = eye + x
    x = _bdot(x, x)
    yield
    for _ in range(2):
        xp = _bdot(x, jnp.concatenate([x, p], axis=1))
        yield
        x, p = xp[:, :n], p + xp[:, n:]
    t = p + _bdot(x, p)
    yield
    for off in (jnp.where(jnp.logical_and(same32, jnp.logical_not(same16)), lmat, 0.0),
                jnp.where(same32, 0.0, lmat)):
        ot = _bdot(off, t)
        yield
        t = t + _bdot(t, ot)
        yield
    return t


def _lockstep(gens):
    while gens:
        alive = []
        for g in gens:
            try:
                next(g)
                alive.append(g)
            except StopIteration:
                pass
        gens = alive


def _wkv_body(r_ref, lw_ref, kt_ref, v_ref, av_ref, bv_ref, s0_ref, y_ref, st_ref, st_s, *, chunk):
    t = pl.program_id(0)

    @pl.when(t == 0)
    def _():
        st_s[...] = s0_ref[...]

    lane = lax.broadcasted_iota(jnp.int32, (1, LANES), 1)
    head_a = lane < HEAD_DIM
    n2 = HEADS_PER_VREG * chunk
    ri = lax.broadcasted_iota(jnp.int32, (n2, n2), 0)
    ci = lax.broadcasted_iota(jnp.int32, (n2, n2), 1)
    strict = ci < ri
    incl = ci <= ri

    def pstack(x):
        return jnp.concatenate([jnp.where(head_a, x, 0.0), jnp.where(head_a, 0.0, x)], axis=0)

    def advance(b, hp, sl):
        cols = slice(hp * LANES, (hp + 1) * LANES)
        r, lw, kt, v, av, bv = (ref[b, sl, cols] for ref in (r_ref, lw_ref, kt_ref, v_ref, av_ref, bv_ref))
        lc = _cumsum_rows(lw)
        lc_last = lc[chunk - 1:chunk, :]
        g_inv = jnp.exp(-lc)
        g_rem = jnp.exp(lc_last - lc)
        lhs = jnp.concatenate([pstack(av * jnp.exp(lc - lw)), pstack(r * jnp.exp(lc))], axis=0).astype(BF16)
        rhs = jnp.concatenate([pstack(bv * g_inv), pstack(kt * g_inv)], axis=0).astype(BF16)
        sc = _dot_nt(lhs, rhs)
        yield
        l_ab = jnp.where(strict, sc[0:n2, 0:n2], 0.0)
        l_ak = jnp.where(strict, sc[0:n2, n2:], 0.0)
        a_rb = jnp.where(incl, sc[n2:, 0:n2], 0.0)
        a_rk = jnp.where(incl, sc[n2:, n2:], 0.0)
        v2 = pstack(v)
        akv = _bdot(l_ak, v2)
        yield
        t_inv = yield from _unit_lower_inverse(l_ab, ri, ci)

        state = st_s[b, hp]
        hs = _dot_nt(lhs, state.astype(BF16))
        yield
        u2 = _bdot(t_inv, hs[0:n2] + akv)
        yield
        uv = jnp.concatenate([u2, v2], axis=0)
        bk = jnp.concatenate([pstack(bv * g_rem), pstack(kt * g_rem)], axis=0)
        st_s[b, hp] = state * jnp.exp(lc_last) + _bdot(uv.T, bk)
        yield
        y2 = hs[n2:] + _bdot(jnp.concatenate([a_rb, a_rk], axis=1), uv)
        y_ref[b, sl, cols] = y2[0:chunk] + y2[chunk:]

    def body(c, carry):
        sl = pl.ds(pl.multiple_of(c * chunk, chunk), chunk)
        _lockstep([advance(b, hp, sl) for b in range(r_ref.shape[0])
                   for hp in range(r_ref.shape[2] // LANES)])
        return carry

    lax.fori_loop(0, r_ref.shape[1] // chunk, body, 0)

    @pl.when(t == pl.num_programs(0) - 1)
    def _():
        st_ref[...] = st_s[...]


def _pair_state(s):
    b, h, n, _ = s.shape
    s = s.reshape(b, h // HEADS_PER_VREG, HEADS_PER_VREG, n, n)
    eye = jnp.eye(HEADS_PER_VREG, dtype=s.dtype)
    return jnp.einsum("bpaij,ac->bpaicj", s, eye).reshape(
        b, h // HEADS_PER_VREG, HEADS_PER_VREG * n, HEADS_PER_VREG * n)


def _unpair_state(s2):
    b, hp, n2, _ = s2.shape
    n = n2 // HEADS_PER_VREG
    s = s2.reshape(b, hp, HEADS_PER_VREG, n, HEADS_PER_VREG, n)
    s = jnp.stack([s[:, :, a, :, a, :] for a in range(HEADS_PER_VREG)], axis=2)
    return s.reshape(b, hp * HEADS_PER_VREG, n, n)


def _wkv_prompt(r, lw, kt, v, av, bv, s0, batch, seq, tb):
    n, w = r.shape
    s2 = _pair_state(s0)
    tok = pl.BlockSpec((batch, tb, w), lambda t: (0, t, 0))
    st = pl.BlockSpec(s2.shape, lambda t: (0, 0, 0, 0))
    seqs = lambda x: x.reshape(batch, seq, w)
    y, s_t = pl.pallas_call(
        functools.partial(_wkv_body, chunk=WKV_CHUNK),
        grid=(seq // tb,),
        in_specs=[tok] * 6 + [st],
        out_specs=[tok, st],
        out_shape=[jax.ShapeDtypeStruct((batch, seq, w), F32), jax.ShapeDtypeStruct(s2.shape, F32)],
        scratch_shapes=[pltpu.VMEM(s2.shape, F32)],
        compiler_params=_params("arbitrary"),
        name="wkv_chunked",
    )(seqs(r), seqs(lw), seqs(kt), seqs(v), seqs(av), seqs(bv), s2)
    return y.reshape(n, w), _unpair_state(s_t)


def _wkv_step_body(r_ref, lw_ref, kt_ref, av_ref, bv_ref, v_ref, s_ref, y_ref, so_ref):
    s = s_ref[...]
    sa = jnp.sum(s * av_ref[...], -1, keepdims=True)
    s = s * jnp.exp(lw_ref[...]) + sa * bv_ref[...] + v_ref[...] * kt_ref[...]
    so_ref[...] = s
    y_ref[...] = jnp.sum(s * r_ref[...], -1, keepdims=True)


def _wkv_step(r, lw, kt, v, av, bv, s0, bb):
    b, h, n, _ = s0.shape
    rowv = lambda x: x.reshape(b, h, 1, n)
    key_spec = pl.BlockSpec((bb, h, 1, n), lambda i: (i, 0, 0, 0))
    val_spec = pl.BlockSpec((bb, h, n, 1), lambda i: (i, 0, 0, 0))
    st_spec = pl.BlockSpec((bb, h, n, n), lambda i: (i, 0, 0, 0))
    y, s_t = pl.pallas_call(
        _wkv_step_body,
        grid=(b // bb,),
        in_specs=[key_spec] * 5 + [val_spec, st_spec],
        out_specs=[val_spec, st_spec],
        out_shape=[jax.ShapeDtypeStruct((b, h, n, 1), F32), jax.ShapeDtypeStruct(s0.shape, F32)],
        compiler_params=_params("parallel"),
        name="wkv_step",
    )(rowv(r), rowv(lw), rowv(kt), rowv(av), rowv(bv), v.reshape(b, h, n, 1), s0)
    return y.reshape(b, h * n), s_t


def _outproj_body(h_ref, oa_ref, y_ref, r_ref, kt_ref, v_ref, g_ref, rk_ref, gw_ref, gb_ref,
                  p_ref, woa_ref, wor_ref, o_ref):
    pmat = p_ref[...]
    y = y_ref[...]
    mean = _split_dot(y, pmat) * (1.0 / HEAD_DIM)
    d = y - mean
    var = _split_dot(d * d, pmat) * (1.0 / HEAD_DIM)
    yn = d * lax.rsqrt(var + GN_EPS) * gw_ref[...] + gb_ref[...]
    bonus = _split_dot(r_ref[...] * kt_ref[...] * rk_ref[...], pmat)
    yo = (yn + bonus * v_ref[...]) * g_ref[...]
    o_ref[...] = (h_ref[...] + _dot(oa_ref[...].astype(BF16), woa_ref[...])
                  + _dot(yo.astype(BF16), wor_ref[...]))


def _outproj(h, o_att, y, r, kt, v, g, lw, tm):
    n, d = h.shape
    row = lambda w: pl.BlockSpec((tm, w), lambda i: (i, 0))
    consts = [lw["r_k"], lw["gn_w"], lw["gn_b"], lw["pmat"], lw["wo_att"], lw["wo_rwkv"]]
    return pl.pallas_call(
        _outproj_body,
        grid=(n // tm,),
        in_specs=[row(d)] + [row(RWKV_W)] * 6 + [_const_spec(c.shape) for c in consts],
        out_specs=row(d),
        out_shape=jax.ShapeDtypeStruct((n, d), F32),
        compiler_params=_params("parallel"),
        name="outproj",
    )(h, o_att, y, r, kt, v, g, *consts)


def _rope_tables(pos):
    half = HEAD_DIM // 2
    inv = ROPE_THETA ** (-jnp.arange(half, dtype=F32) / half)
    ang = pos.astype(F32)[:, None] * inv[None, :]
    cos, sin = jnp.cos(ang), jnp.sin(ang)
    zero = jnp.zeros_like(sin)
    per_head = (jnp.concatenate([cos, cos], 1), jnp.concatenate([-sin, zero], 1),
                jnp.concatenate([zero, sin], 1))
    return tuple(jnp.concatenate([t] * HEADS_PER_VREG, 1) for t in per_head)


def _pad_rows(x, rows, at):
    return jnp.zeros((rows, x.shape[1]), x.dtype).at[at:at + x.shape[0]].set(x)


def _ffn_weights(wg, wu, wd):
    assert wg.shape[1] % FF_CHUNK == 0
    return wg.astype(BF16), wu.astype(BF16), wd.astype(BF16)


def _layer_weights(l, p):
    w_in = p["w_in"][l]
    rw0 = 3 * ATT_W
    lo0 = rw0 + 3 * RWKV_W
    row = lambda x: x.reshape(1, -1)
    head_ones = jnp.kron(jnp.eye(N_RWKV_HEADS, dtype=F32), jnp.ones((HEAD_DIM, HEAD_DIM), F32))
    lw = {
        "ffn1": _ffn_weights(p["ffn1_gate"][l], p["ffn1_up"][l], p["ffn1_down"][l]),
        "ffn2": _ffn_weights(p["ffn2_gate"][l], p["ffn2_up"][l], p["ffn2_down"][l]),
        "ffn1_norm": row(p["ffn1_norm"][l]),
        "ffn2_norm": row(p["ffn2_norm"][l]),
        "mix_norm": row(p["mix_norm"][l]),
        "wqkv": w_in[:, :rw0].astype(BF16),
        "wr": w_in[:, rw0:lo0].astype(BF16),
        "wl": jnp.pad(w_in[:, lo0:], ((0, 0), (0, LORA_PAD - LORA_W))).astype(BF16),
        "mu_r": row(p["rwkv_mu"][l][:3 * RWKV_W]),
        "mu_l": row(jnp.pad(p["rwkv_mu"][l][3 * RWKV_W:], (0, LORA_PAD - LORA_W))),
        "w0": row(p["rwkv_w0"][l]),
        "wdec": _pad_rows(p["rwkv_decay_b"][l], LORA_PAD, 0).astype(BF16),
        "a0": row(p["rwkv_a0"][l]),
        "wa": _pad_rows(p["rwkv_a_b"][l], LORA_PAD, DECAY_LORA).astype(BF16),
        "wg": _pad_rows(p["rwkv_g_b"][l], LORA_PAD, DECAY_LORA + AAA_LORA).astype(BF16),
        "k_k": row(p["rwkv_k_k"][l]),
        "k_a": row(p["rwkv_k_a"][l]),
        "r_k": row(p["rwkv_r_k"][l]),
        "gn_w": row(p["rwkv_gn_w"][l]),
        "gn_b": row(p["rwkv_gn_b"][l]),
        "pmat": head_ones.astype(BF16),
        "wo_att": p["w_out"][l][:ATT_W].astype(BF16),
        "wo_rwkv": p["w_out"][l][ATT_W:].astype(BF16),
    }
    if l > 0:
        lw["v0"] = row(p["rwkv_v0"][l - 1])
        lw["v_a"] = jnp.pad(p["rwkv_v_a"][l - 1], ((0, 0), (0, MV_PAD - MV_LORA))).astype(BF16)
        lw["v_b"] = _pad_rows(p["rwkv_v_b"][l - 1], MV_PAD, 0).astype(BF16)
    return lw


def _split_shift(shift):
    return shift[:, :3 * RWKV_W], jnp.pad(shift[:, 3 * RWKV_W:], ((0, 0), (0, LORA_PAD - LORA_W)))


def _run(x, layers, final_g, tabs, tm, mix):
    h = x
    v_first = None
    new_k, new_v, new_wkv, new_shift = [], [], [], []
    depth = len(layers)
    for l, lw in enumerate(layers):
        h = _ffn(h, lw["ffn1_norm"], *lw["ffn1"], tm)
        q, k, v, zr, zl = _inproj(h, lw["mix_norm"], lw["wqkv"], lw["wr"], lw["wl"], tabs, tm)
        o_att, (r, kt, vv, g, y), wkv_t, keep_k, keep_v, shift_t = mix(l, lw, q, k, v, zr, zl, v_first)
        if l == 0:
            v_first = vv
        h = _outproj(h, o_att, y, r, kt, vv, g, lw, tm)
        h = _ffn(h, lw["ffn2_norm"], *lw["ffn2"], tm, final_g if l == depth - 1 else None)
        new_k.append(keep_k)
        new_v.append(keep_v)
        new_wkv.append(wkv_t)
        new_shift.append(shift_t)
    return h, jnp.stack(new_k), jnp.stack(new_v), jnp.stack(new_wkv), jnp.stack(new_shift)


def kernel(x_prompt, x_sample, cache_k, cache_v, state_wkv, state_shift, ffn1_norm, ffn1_gate, ffn1_up, ffn1_down, mix_norm, w_in, rwkv_mu, rwkv_w0, rwkv_decay_b, rwkv_a0, rwkv_a_b, rwkv_g_b, rwkv_k_k, rwkv_k_a, rwkv_r_k, rwkv_gn_w, rwkv_gn_b, rwkv_v0, rwkv_v_a, rwkv_v_b, w_out, ffn2_norm, ffn2_gate, ffn2_up, ffn2_down, final_norm):
    p = dict(ffn1_norm=ffn1_norm, ffn1_gate=ffn1_gate, ffn1_up=ffn1_up, ffn1_down=ffn1_down,
             mix_norm=mix_norm, w_in=w_in, rwkv_mu=rwkv_mu, rwkv_w0=rwkv_w0,
             rwkv_decay_b=rwkv_decay_b, rwkv_a0=rwkv_a0, rwkv_a_b=rwkv_a_b, rwkv_g_b=rwkv_g_b,
             rwkv_k_k=rwkv_k_k, rwkv_k_a=rwkv_k_a, rwkv_r_k=rwkv_r_k, rwkv_gn_w=rwkv_gn_w,
             rwkv_gn_b=rwkv_gn_b, rwkv_v0=rwkv_v0, rwkv_v_a=rwkv_v_a, rwkv_v_b=rwkv_v_b,
             w_out=w_out, ffn2_norm=ffn2_norm, ffn2_gate=ffn2_gate, ffn2_up=ffn2_up,
             ffn2_down=ffn2_down)
    depth = w_in.shape[0]
    layers = [_layer_weights(l, p) for l in range(depth)]
    final_g = final_norm.reshape(1, -1)
    d_model = x_prompt.shape[-1]
    cols = 3 * RWKV_W + LORA_W

    def shift_out(zr, zl, batch):
        last = lambda z: z.reshape(batch, -1, z.shape[-1])[:, -1]
        return jnp.concatenate([last(zr), last(zl)[:, :LORA_W]], axis=-1)

    bp, sp, _ = x_prompt.shape
    keep = min(MAX_WINDOW, sp)
    tm_p = 512
    zero_shift = _split_shift(jnp.zeros((bp, cols), F32))
    zero_wkv = jnp.zeros((bp, N_RWKV_HEADS, HEAD_DIM, HEAD_DIM), F32)

    def heads(t, width):
        window = t.reshape(bp, sp, width)[:, sp - keep:]
        return window.reshape(bp, keep, width // HEAD_DIM, HEAD_DIM)

    def prompt_mix(l, lw, q, k, v, zr, zl, v_first):
        o_att = _attn_prompt(q, k, v, bp, sp)
        r, lwd, kt, vv, av, bv, g = _prep(zr, zl, lw, tm_p, shift0=zero_shift, seq=sp,
                                          v_first=v_first if l > 0 else None)
        y, wkv_t = _wkv_prompt(r, lwd, kt, vv, av, bv, zero_wkv, bp, sp, tm_p)
        return (o_att, (r, kt, vv, g, y), wkv_t, heads(k, ATT_W), heads(v, ATT_W),
                shift_out(zr, zl, bp))

    y_p, k_p, v_p, wkv_p, shift_p = _run(
        x_prompt.reshape(bp * sp, d_model), layers, final_g, _rope_tables(jnp.arange(sp)), tm_p,
        prompt_mix)

    bs, ts, _ = x_sample.shape
    assert ts == 1
    pos_s = jnp.tile(PAST_LEN + jnp.arange(ts), bs)

    def sample_mix(l, lw, q, k, v, zr, zl, v_first):
        o_att = _attn_sample(l, q, k, v, cache_k, cache_v)
        r, lwd, kt, vv, av, bv, g = _prep(zr, zl, lw, bs, prev_rows=_split_shift(state_shift[l]),
                                          v_first=v_first if l > 0 else None)
        y, wkv_t = _wkv_step(r, lwd, kt, vv, av, bv, state_wkv[l], SUBLANES)
        new_rows = lambda t: t.reshape(bs, ts, N_ATT_HEADS, HEAD_DIM)
        return (o_att, (r, kt, vv, g, y), wkv_t, new_rows(k), new_rows(v),
                shift_out(zr, zl, bs))

    y_s, k_s, v_s, wkv_s, shift_s = _run(
        x_sample.reshape(bs * ts, d_model), layers, final_g, _rope_tables(pos_s), bs * ts,
        sample_mix)

    return (y_p.reshape(bp, sp, d_model), y_s.reshape(bs, ts, d_model), k_p, v_p, wkv_p, shift_p,
            k_s, v_s, wkv_s, shift_s)
```

```python
import functools

import jax
import jax.numpy as jnp
from jax import lax
from jax.experimental import pallas as pl
from jax.experimental.pallas import tpu as pltpu

F32 = jnp.float32
BF16 = jnp.bfloat16

HEAD_DIM = 64
N_ATT_HEADS = 8
N_RWKV_HEADS = 8
ATT_W = N_ATT_HEADS * HEAD_DIM
RWKV_W = N_RWKV_HEADS * HEAD_DIM
BRANCHES = ((128, 1), (512, 4), (2048, 16))
MAX_WINDOW = 2048
Q_BLK = 128
ROPE_THETA = 10000.0
DECAY_LORA = 64
AAA_LORA = 64
MV_LORA = 32
GATE_LORA = 160
LORA_W = DECAY_LORA + AAA_LORA + GATE_LORA
PAST_LEN = 8192
RMS_EPS = 1e-6
GN_EPS = 64e-5

LANES = 128
SUBLANES = 8
LORA_PAD = -(-LORA_W // LANES) * LANES
MV_PAD = LANES
HEADS_PER_VREG = LANES // HEAD_DIM
N_HEAD_PAIRS = RWKV_W // LANES
VMEM_LIMIT = 56 * 1024 * 1024
FF_CHUNK = 256
WKV_CHUNK = 64
ATT_TILE = MAX_WINDOW
ATT_UNROLL = 4


def _dot(a, b):
    return jnp.dot(a, b, preferred_element_type=F32)


def _dot_nt(a, b):
    return lax.dot_general(a, b, (((1,), (1,)), ((), ())), preferred_element_type=F32)


def _bdot(a, b):
    return _dot(a.astype(BF16), b.astype(BF16))


def _split_dot(x, p):
    hi = x.astype(BF16)
    lo = (x - hi.astype(F32)).astype(BF16)
    return _dot(hi, p) + _dot(lo, p)


def _rms(x, g):
    return x * lax.rsqrt(jnp.mean(x * x, -1, keepdims=True) + RMS_EPS) * g


def _params(*sem):
    return pltpu.CompilerParams(dimension_semantics=sem, vmem_limit_bytes=VMEM_LIMIT)


def _const_spec(shape):
    nd = len(shape)
    return pl.BlockSpec(shape, lambda *_: (0,) * nd, pipeline_mode=pl.Buffered(1))


def _ffn_body(h_ref, g_ref, wg_ref, wu_ref, wd_ref, *rest, final):
    if final:
        fg_ref, o_ref, n_s, acc_s = rest
    else:
        o_ref, n_s, acc_s = rest
    x = h_ref[...]
    n_s[...] = _rms(x, g_ref[...]).astype(BF16)
    acc_s[...] = jnp.zeros_like(acc_s)

    for c in range(wg_ref.shape[1] // FF_CHUNK):
        cols = slice(c * FF_CHUNK, (c + 1) * FF_CHUNK)
        n = n_s[...]
        gt = _dot(n, wg_ref[:, cols])
        up = _dot(n, wu_ref[:, cols])
        act = (gt * jax.nn.sigmoid(gt) * up).astype(BF16)
        acc_s[...] += _dot(act, wd_ref[cols, :])
    y = x + 0.5 * acc_s[...]
    if final:
        y = _rms(y, fg_ref[...])
    o_ref[...] = y


def _layer_spec(stacked, l):
    return pl.BlockSpec((None,) + stacked.shape[1:], lambda *_: (l, 0, 0),
                        pipeline_mode=pl.Buffered(1))


def _ffn(h, g, l, wg_all, wu_all, wd_all, tm, final_g=None):
    n, d = h.shape
    row = pl.BlockSpec((tm, d), lambda i: (i, 0))
    in_specs = [row, _const_spec((1, d)), _layer_spec(wg_all, l), _layer_spec(wu_all, l),
                _layer_spec(wd_all, l)]
    args = [h, g, wg_all, wu_all, wd_all]
    if final_g is not None:
        in_specs.append(_const_spec((1, d)))
        args.append(final_g)
    return pl.pallas_call(
        functools.partial(_ffn_body, final=final_g is not None),
        grid=(n // tm,),
        in_specs=in_specs,
        out_specs=row,
        out_shape=jax.ShapeDtypeStruct((n, d), F32),
        scratch_shapes=[pltpu.VMEM((tm, d), BF16), pltpu.VMEM((tm, d), F32)],
        compiler_params=_params("parallel"),
        name="ffn",
    )(*args)


def _rope(x, c, s1, s2):
    reps = x.shape[1] // LANES
    c, s1, s2 = (jnp.concatenate([t] * reps, axis=1) for t in (c, s1, s2))
    half = HEAD_DIM // 2
    upper = pltpu.roll(x, x.shape[1] - half, axis=1)
    lower = pltpu.roll(x, half, axis=1)
    return x * c + upper * s1 + lower * s2


def _inproj_body(h_ref, g_ref, wqkv_ref, wr_ref, wl_ref, c_ref, s1_ref, s2_ref,
                 q_ref, k_ref, v_ref, zr_ref, zl_ref):
    n = _rms(h_ref[...], g_ref[...]).astype(BF16)
    c, s1, s2 = c_ref[...], s1_ref[...], s2_ref[...]
    q = _dot(n, wqkv_ref[:, 0:ATT_W])
    q_ref[...] = _rope(q, c, s1, s2) * (HEAD_DIM ** -0.5)
    k = _dot(n, wqkv_ref[:, ATT_W:2 * ATT_W])
    k_ref[...] = _rope(k, c, s1, s2)
    v_ref[...] = _dot(n, wqkv_ref[:, 2 * ATT_W:3 * ATT_W])
    zr_ref[...] = _dot(n, wr_ref[...])
    zl_ref[...] = _dot(n, wl_ref[...])


def _inproj(h, g, wqkv, wr, wl, tabs, tm):
    n, d = h.shape
    tab_blocks = tabs[0].shape[0] // tm
    row = lambda w: pl.BlockSpec((tm, w), lambda i: (i, 0))
    tab = pl.BlockSpec((tm, LANES), lambda i: (i % tab_blocks, 0))
    widths = (ATT_W, ATT_W, ATT_W, 3 * RWKV_W, LORA_PAD)
    return pl.pallas_call(
        _inproj_body,
        grid=(n // tm,),
        in_specs=[row(d), _const_spec((1, d)), _const_spec(wqkv.shape), _const_spec(wr.shape),
                  _const_spec(wl.shape), tab, tab, tab],
        out_specs=[row(w) for w in widths],
        out_shape=[jax.ShapeDtypeStruct((n, w), F32) for w in widths],
        compiler_params=_params("parallel"),
        name="inproj_rope",
    )(h, g, wqkv, wr, wl, *tabs)


def _attn_body(q_ref, kp_ref, kc_ref, vp_ref, vc_ref, o_ref, ks, vs, acc_s, m_s, den_s, *, tq):
    t = pl.program_id(2)
    ks[0:tq, :] = kp_ref[...]
    ks[tq:2 * tq, :] = kc_ref[...]
    vs[0:tq, :] = vp_ref[...]
    vs[tq:2 * tq, :] = vc_ref[...]
    lane = lax.broadcasted_iota(jnp.int32, (1, LANES), 1)
    head_a = lane < HEAD_DIM
    row = lax.broadcasted_iota(jnp.int32, (Q_BLK, 2 * Q_BLK), 0)
    col = lax.broadcasted_iota(jnp.int32, (Q_BLK, 2 * Q_BLK), 1)

    for bi, (window, dil) in enumerate(BRANCHES):
        n_steps = window // dil
        band = (col >= row + (Q_BLK - n_steps)) & (col <= row + Q_BLK)
        span = Q_BLK * dil

        def block(idx, bi=bi, dil=dil, band=band, span=span):
            sb = idx // dil
            res = idx - sb * dil
            qbase = sb * span + res
            if dil == 1:
                qsl = pl.ds(pl.multiple_of(qbase, Q_BLK), Q_BLK)
                ksl = pl.ds(pl.multiple_of(tq + qbase - span, Q_BLK), 2 * Q_BLK)
            else:
                qsl = pl.ds(qbase, Q_BLK, stride=dil)
                ksl = pl.ds(tq + qbase - span, 2 * Q_BLK, stride=dil)
            q = q_ref[qsl, :]
            kb = ks[ksl, :].astype(BF16)
            vb = vs[ksl, :].astype(BF16)
            cmin = jnp.where(jnp.logical_and(t == 0, sb == 0), Q_BLK, 0)
            mask = band & (col >= cmin)
            scores = []
            for hm in (head_a, jnp.logical_not(head_a)):
                scores.append(_dot_nt(jnp.where(hm, q, 0.0).astype(BF16), kb))
                yield
            parts = []
            for s in scores:
                s = jnp.where(mask, s, -jnp.inf)
                m = jnp.max(s, -1, keepdims=True)
                p = jnp.exp(s - m)
                den = jnp.sum(p, -1, keepdims=True)
                parts.append((_dot(p.astype(BF16), vb), m, den))
                yield
            (pa, ma, da), (pb, mb, db) = parts
            acc_s[bi, qsl, :] = jnp.where(head_a, pa, pb)
            m_s[bi, qsl, :] = jnp.where(head_a, ma, mb)
            den_s[bi, qsl, :] = jnp.where(head_a, da, db)

        def body(i, carry, block=block):
            _lockstep([block(i * ATT_UNROLL + u) for u in range(ATT_UNROLL)])
            return carry

        lax.fori_loop(0, tq // Q_BLK // ATT_UNROLL, body, 0)

    m_all = [m_s[i] for i in range(len(BRANCHES))]
    m_max = functools.reduce(jnp.maximum, m_all)
    wts = [jnp.exp(m - m_max) for m in m_all]
    num = sum(w * acc_s[i] for i, w in enumerate(wts))
    tot = sum(w * den_s[i] for i, w in enumerate(wts))
    o_ref[...] = num / tot


def _attn_prompt(q, k, v, batch, seq):
    n, w = q.shape
    tq = ATT_TILE
    assert seq % tq == 0
    nt = seq // tq
    cur = pl.BlockSpec((tq, LANES), lambda b, hp, t: (b * nt + t, hp))
    prev = pl.BlockSpec((tq, LANES), lambda b, hp, t: (b * nt + jnp.maximum(t - 1, 0), hp))
    nb = len(BRANCHES)
    return pl.pallas_call(
        functools.partial(_attn_body, tq=tq),
        grid=(batch, w // LANES, nt),
        in_specs=[cur, prev, cur, prev, cur],
        out_specs=cur,
        out_shape=jax.ShapeDtypeStruct((n, w), F32),
        scratch_shapes=[pltpu.VMEM((2 * tq, LANES), F32), pltpu.VMEM((2 * tq, LANES), F32),
                        pltpu.VMEM((nb, tq, LANES), F32), pltpu.VMEM((nb, tq, LANES), F32),
                        pltpu.VMEM((nb, tq, LANES), F32)],
        compiler_params=_params("parallel", "parallel", "arbitrary"),
        name="attn_prompt",
    )(q, k, k, v, v)


def _attn_sample_body(qt_ref, knt_ref, vnt_ref, kt_ref, vt_ref, o_ref):
    qt, knt, vnt = qt_ref[0], knt_ref[0], vnt_ref[0]
    n_heads, _, buf_len = kt_ref.shape
    pos = lax.broadcasted_iota(jnp.int32, (1, buf_len), 1)
    dist = buf_len - pos
    cnt = sum((jnp.logical_and(lax.rem(dist, dil) == 0, dist <= window)).astype(F32)
              for window, dil in BRANCHES)
    live = cnt > 0.0
    head_lane = lax.broadcasted_iota(jnp.int32, (1, n_heads), 1)
    s_new_all = jnp.sum(qt * knt, 0, keepdims=True)
    out = jnp.zeros_like(qt)
    for h in range(n_heads):
        pick = lambda x: jnp.sum(jnp.where(head_lane == h, x, 0.0), 1, keepdims=True)
        s = jnp.sum(pick(qt) * kt_ref[h], 0, keepdims=True)
        s_new = pick(s_new_all)
        m = jnp.maximum(jnp.max(jnp.where(live, s, -jnp.inf), -1, keepdims=True), s_new)
        c = jnp.where(live, cnt * jnp.exp(s - m), 0.0)
        c_new = len(BRANCHES) * jnp.exp(s_new - m)
        tot = jnp.sum(c, -1, keepdims=True) + c_new
        num = jnp.sum(c * vt_ref[h], -1, keepdims=True) + c_new * pick(vnt)
        out = jnp.where(head_lane == h, num / tot, out)
    o_ref[0] = out


def _attn_sample(l, q, kn, vn, cache_k, cache_v):
    depth, b, buf_len, h, e = cache_k.shape
    assert all(buf_len >= window for window, _ in BRANCHES)
    by_pos = lambda c: jnp.transpose(c, (0, 1, 3, 4, 2))
    cache_spec = pl.BlockSpec((None, None, h, e, buf_len), lambda i: (l, i, 0, 0, 0))
    row = pl.BlockSpec((1, e, h), lambda i: (i, 0, 0))
    cols = lambda t: t.reshape(b, h, e).transpose(0, 2, 1)
    out = pl.pallas_call(
        _attn_sample_body,
        grid=(b,),
        in_specs=[row, row, row, cache_spec, cache_spec],
        out_specs=row,
        out_shape=jax.ShapeDtypeStruct((b, e, h), F32),
        compiler_params=_params("parallel"),
        name="attn_sample",
    )(cols(q), cols(kn), cols(vn), by_pos(cache_k), by_pos(cache_v))
    return out.transpose(0, 2, 1).reshape(b, h * e)


def _prep_body(*refs, has_vres, rowwise_prev, tiles_per_seq):
    it = iter(refs)
    zr_ref, zl_ref = next(it), next(it)
    if rowwise_prev:
        zpr, zpl = next(it)[...], next(it)[...]
        zr, zl = zr_ref[...], zl_ref[...]
    else:
        pr8_ref, pl8_ref, s0r_ref, s0l_ref = next(it), next(it), next(it), next(it)
        zr, zl = zr_ref[...], zl_ref[...]
        is_start = (pl.program_id(0) % tiles_per_seq) == 0
        row0 = lax.broadcasted_iota(jnp.int32, (zr.shape[0], 1), 0) == 0
        last = SUBLANES - 1
        prev_r = jnp.where(is_start, s0r_ref[0], pr8_ref[last:last + 1, :])
        prev_l = jnp.where(is_start, s0l_ref[0], pl8_ref[last:last + 1, :])
        zpr = jnp.where(row0, prev_r, pltpu.roll(zr, 1, axis=0))
        zpl = jnp.where(row0, prev_l, pltpu.roll(zl, 1, axis=0))
    mur, mul, w0, wdec, a0, wa, wg, k_k, k_a, pmat = (next(it) for _ in range(10))
    if has_vres:
        vf_ref, v0, va, vb = next(it), next(it), next(it), next(it)
    r_o, lw_o, kt_o, v_o, av_o, bv_o, g_o = (next(it) for _ in range(7))

    zmr = zr + (zpr - zr) * mur[...]
    zml = zl + (zpl - zl) * mul[...]
    r = zmr[:, 0:RWKV_W]
    k = zmr[:, RWKV_W:2 * RWKV_W]
    v = zmr[:, 2 * RWKV_W:3 * RWKV_W]
    y = w0[...] + _dot(jnp.tanh(zml).astype(BF16), wdec[...])
    w_log = -(jnp.maximum(-y, 0.0) + jnp.log1p(jnp.exp(-jnp.abs(y)))) - 0.5
    lw_o[...] = -jnp.exp(w_log)
    a = jax.nn.sigmoid(a0[...] + _dot(zml.astype(BF16), wa[...]))
    g_o[...] = _dot(jax.nn.sigmoid(zml).astype(BF16), wg[...])
    if has_vres:
        mix = jax.nn.sigmoid(v0[...] + _dot(_dot(v.astype(BF16), va[...]).astype(BF16), vb[...]))
        v = v + (vf_ref[...] - v) * mix
    kk = k * k_k[...]
    norm = jnp.sqrt(_split_dot(kk * kk, pmat[...]))
    kk = kk / jnp.maximum(norm, 1e-12)
    r_o[...] = r
    kt_o[...] = k * (1.0 + (a - 1.0) * k_a[...])
    v_o[...] = v
    av_o[...] = -kk
    bv_o[...] = kk * a


def _prep(zr, zl, lw, tm, *, prev_rows=None, shift0=None, seq=None, v_first=None):
    n = zr.shape[0]
    row = lambda w: pl.BlockSpec((tm, w), lambda i: (i, 0))
    args, in_specs = [zr, zl], [row(3 * RWKV_W), row(LORA_PAD)]
    if prev_rows is not None:
        args += list(prev_rows)
        in_specs += [row(3 * RWKV_W), row(LORA_PAD)]
        tiles_per_seq = None
    else:
        tiles_per_seq = seq // tm
        t8 = tm // SUBLANES
        prev8 = lambda w: pl.BlockSpec((SUBLANES, w), lambda i: (jnp.maximum(i * t8 - 1, 0), 0))
        start = lambda w: pl.BlockSpec((1, 1, w), lambda i: (i // tiles_per_seq, 0, 0))
        args += [zr, zl, shift0[0][:, None], shift0[1][:, None]]
        in_specs += [prev8(3 * RWKV_W), prev8(LORA_PAD), start(3 * RWKV_W), start(LORA_PAD)]
    consts = [lw["mu_r"], lw["mu_l"], lw["w0"], lw["wdec"], lw["a0"], lw["wa"], lw["wg"],
              lw["k_k"], lw["k_a"], lw["pmat"]]
    args += consts
    in_specs += [_const_spec(c.shape) for c in consts]
    if v_first is not None:
        vres = [lw["v0"], lw["v_a"], lw["v_b"]]
        args += [v_first] + vres
        in_specs += [row(RWKV_W)] + [_const_spec(c.shape) for c in vres]
    return pl.pallas_call(
        functools.partial(_prep_body, has_vres=v_first is not None,
                          rowwise_prev=prev_rows is not None, tiles_per_seq=tiles_per_seq),
        grid=(n // tm,),
        in_specs=in_specs,
        out_specs=[row(RWKV_W)] * 7,
        out_shape=[jax.ShapeDtypeStruct((n, RWKV_W), F32)] * 7,
        compiler_params=_params("parallel"),
        name="rwkv_premix",
    )(*args)


def _cumsum_rows(x):
    rows = lax.broadcasted_iota(jnp.int32, (x.shape[0], 1), 0)
    s = 1
    while s < x.shape[0]:
        x = x + jnp.where(rows >= s, pltpu.roll(x, s, axis=0), 0.0)
        s *= 2
    return x


def _unit_lower_inverse(lmat, ri, ci):
    blk = lambda sh: (ri >> sh) == (ci >> sh)
    same16, same32 = blk(4), blk(5)
    eye = (ri == ci).astype(F32)
    n = lmat.shape[0]
    x = jnp.where(same16, lmat, 0.0)
    p = eye + x
    x = _bdot(x, x)
    yield
    for _ in range(2):
        xp = _bdot(x, jnp.concatenate([x, p], axis=1))
        yield
        x, p = xp[:, :n], p + xp[:, n:]
    t = p + _bdot(x, p)
    yield
    for off in (jnp.where(jnp.logical_and(same32, jnp.logical_not(same16)), lmat, 0.0),
                jnp.where(same32, 0.0, lmat)):
        ot = _bdot(off, t)
        yield
        t = t + _bdot(t, ot)
        yield
    return t


def _lockstep(gens):
    while gens:
        alive = []
        for g in gens:
            try:
                next(g)
                alive.append(g)
            except StopIteration:
                pass
        gens = alive


def _wkv_body(r_ref, lw_ref, kt_ref, v_ref, av_ref, bv_ref, s0_ref, y_ref, st_ref, st_s, *, chunk):
    t = pl.program_id(0)

    @pl.when(t == 0)
    def _():
        st_s[...] = s0_ref[...]

    lane = lax.broadcasted_iota(jnp.int32, (1, LANES), 1)
    head_a = lane < HEAD_DIM
    n2 = HEADS_PER_VREG * chunk
    ri = lax.broadcasted_iota(jnp.int32, (n2, n2), 0)
    ci = lax.broadcasted_iota(jnp.int32, (n2, n2), 1)
    strict = ci < ri
    incl = ci <= ri

    def pstack(x):
        return jnp.concatenate([jnp.where(head_a, x, 0.0), jnp.where(head_a, 0.0, x)], axis=0)

    def advance(b, hp, sl):
        cols = slice(hp * LANES, (hp + 1) * LANES)
        r, lw, kt, v, av, bv = (ref[b, sl, cols] for ref in (r_ref, lw_ref, kt_ref, v_ref, av_ref, bv_ref))
        lc = _cumsum_rows(lw)
        lc_last = lc[chunk - 1:chunk, :]
        g_inv = jnp.exp(-lc)
        g_rem = jnp.exp(lc_last - lc)
        lhs = jnp.concatenate([pstack(av * jnp.exp(lc - lw)), pstack(r * jnp.exp(lc))], axis=0).astype(BF16)
        rhs = jnp.concatenate([pstack(bv * g_inv), pstack(kt * g_inv)], axis=0).astype(BF16)
        sc = _dot_nt(lhs, rhs)
        yield
        l_ab = jnp.where(strict, sc[0:n2, 0:n2], 0.0)
        l_ak = jnp.where(strict, sc[0:n2, n2:], 0.0)
        a_rb = jnp.where(incl, sc[n2:, 0:n2], 0.0)
        a_rk = jnp.where(incl, sc[n2:, n2:], 0.0)
        v2 = pstack(v)
        akv = _bdot(l_ak, v2)
        yield
        t_inv = yield from _unit_lower_inverse(l_ab, ri, ci)

        state = st_s[b, hp]
        hs = _dot_nt(lhs, state.astype(BF16))
        yield
        u2 = _bdot(t_inv, hs[0:n2] + akv)
        yield
        uv = jnp.concatenate([u2, v2], axis=0)
        bk = jnp.concatenate([pstack(bv * g_rem), pstack(kt * g_rem)], axis=0)
        st_s[b, hp] = state * jnp.exp(lc_last) + _bdot(uv.T, bk)
        yield
        y2 = hs[n2:] + _bdot(jnp.concatenate([a_rb, a_rk], axis=1), uv)
        y_ref[b, sl, cols] = y2[0:chunk] + y2[chunk:]

    def body(c, carry):
        sl = pl.ds(pl.multiple_of(c * chunk, chunk), chunk)
        _lockstep([advance(b, hp, sl) for b in range(r_ref.shape[0])
                   for hp in range(r_ref.shape[2] // LANES)])
        return carry

    lax.fori_loop(0, r_ref.shape[1] // chunk, body, 0)

    @pl.when(t == pl.num_programs(0) - 1)
    def _():
        st_ref[...] = st_s[...]


def _pair_state(s):
    b, h, n, _ = s.shape
    s = s.reshape(b, h // HEADS_PER_VREG, HEADS_PER_VREG, n, n)
    eye = jnp.eye(HEADS_PER_VREG, dtype=s.dtype)
    return jnp.einsum("bpaij,ac->bpaicj", s, eye).reshape(
        b, h // HEADS_PER_VREG, HEADS_PER_VREG * n, HEADS_PER_VREG * n)


def _unpair_state(s2):
    b, hp, n2, _ = s2.shape
    n = n2 // HEADS_PER_VREG
    s = s2.reshape(b, hp, HEADS_PER_VREG, n, HEADS_PER_VREG, n)
    s = jnp.stack([s[:, :, a, :, a, :] for a in range(HEADS_PER_VREG)], axis=2)
    return s.reshape(b, hp * HEADS_PER_VREG, n, n)


def _wkv_prompt(r, lw, kt, v, av, bv, s0, batch, seq, tb):
    n, w = r.shape
    s2 = _pair_state(s0)
    tok = pl.BlockSpec((batch, tb, w), lambda t: (0, t, 0))
    st = pl.BlockSpec(s2.shape, lambda t: (0, 0, 0, 0))
    seqs = lambda x: x.reshape(batch, seq, w)
    y, s_t = pl.pallas_call(
        functools.partial(_wkv_body, chunk=WKV_CHUNK),
        grid=(seq // tb,),
        in_specs=[tok] * 6 + [st],
        out_specs=[tok, st],
        out_shape=[jax.ShapeDtypeStruct((batch, seq, w), F32), jax.ShapeDtypeStruct(s2.shape, F32)],
        scratch_shapes=[pltpu.VMEM(s2.shape, F32)],
        compiler_params=_params("arbitrary"),
        name="wkv_chunked",
    )(seqs(r), seqs(lw), seqs(kt), seqs(v), seqs(av), seqs(bv), s2)
    return y.reshape(n, w), _unpair_state(s_t)


def _wkv_step_body(r_ref, lw_ref, kt_ref, av_ref, bv_ref, v_ref, s_ref, y_ref, so_ref):
    s = s_ref[...]
    sa = jnp.sum(s * av_ref[...], -1, keepdims=True)
    s = s * jnp.exp(lw_ref[...]) + sa * bv_ref[...] + v_ref[...] * kt_ref[...]
    so_ref[...] = s
    y_ref[...] = jnp.sum(s * r_ref[...], -1, keepdims=True)


def _wkv_step(r, lw, kt, v, av, bv, s0, bb):
    b, h, n, _ = s0.shape
    rowv = lambda x: x.reshape(b, h, 1, n)
    key_spec = pl.BlockSpec((bb, h, 1, n), lambda i: (i, 0, 0, 0))
    val_spec = pl.BlockSpec((bb, h, n, 1), lambda i: (i, 0, 0, 0))
    st_spec = pl.BlockSpec((bb, h, n, n), lambda i: (i, 0, 0, 0))
    y, s_t = pl.pallas_call(
        _wkv_step_body,
        grid=(b // bb,),
        in_specs=[key_spec] * 5 + [val_spec, st_spec],
        out_specs=[val_spec, st_spec],
        out_shape=[jax.ShapeDtypeStruct((b, h, n, 1), F32), jax.ShapeDtypeStruct(s0.shape, F32)],
        compiler_params=_params("parallel"),
        name="wkv_step",
    )(rowv(r), rowv(lw), rowv(kt), rowv(av), rowv(bv), v.reshape(b, h, n, 1), s0)
    return y.reshape(b, h * n), s_t


def _outproj_body(h_ref, oa_ref, y_ref, r_ref, kt_ref, v_ref, g_ref, rk_ref, gw_ref, gb_ref,
                  p_ref, woa_ref, wor_ref, o_ref):
    pmat = p_ref[...]
    y = y_ref[...]
    mean = _split_dot(y, pmat) * (1.0 / HEAD_DIM)
    d = y - mean
    var = _split_dot(d * d, pmat) * (1.0 / HEAD_DIM)
    yn = d * lax.rsqrt(var + GN_EPS) * gw_ref[...] + gb_ref[...]
    bonus = _split_dot(r_ref[...] * kt_ref[...] * rk_ref[...], pmat)
    yo = (yn + bonus * v_ref[...]) * g_ref[...]
    o_ref[...] = (h_ref[...] + _dot(oa_ref[...].astype(BF16), woa_ref[...])
                  + _dot(yo.astype(BF16), wor_ref[...]))


def _outproj(h, o_att, y, r, kt, v, g, lw, tm):
    n, d = h.shape
    row = lambda w: pl.BlockSpec((tm, w), lambda i: (i, 0))
    consts = [lw["r_k"], lw["gn_w"], lw["gn_b"], lw["pmat"], lw["wo_att"], lw["wo_rwkv"]]
    return pl.pallas_call(
        _outproj_body,
        grid=(n // tm,),
        in_specs=[row(d)] + [row(RWKV_W)] * 6 + [_const_spec(c.shape) for c in consts],
        out_specs=row(d),
        out_shape=jax.ShapeDtypeStruct((n, d), F32),
        compiler_params=_params("parallel"),
        name="outproj",
    )(h, o_att, y, r, kt, v, g, *consts)


def _rope_tables(pos):
    half = HEAD_DIM // 2
    inv = ROPE_THETA ** (-jnp.arange(half, dtype=F32) / half)
    ang = pos.astype(F32)[:, None] * inv[None, :]
    cos, sin = jnp.cos(ang), jnp.sin(ang)
    zero = jnp.zeros_like(sin)
    per_head = (jnp.concatenate([cos, cos], 1), jnp.concatenate([-sin, zero], 1),
                jnp.concatenate([zero, sin], 1))
    return tuple(jnp.concatenate([t] * HEADS_PER_VREG, 1) for t in per_head)


def _pad_rows(x, rows, at):
    return jnp.zeros((rows, x.shape[1]), x.dtype).at[at:at + x.shape[0]].set(x)


def _ffn_weights(wg, wu, wd):
    assert wg.shape[2] % FF_CHUNK == 0
    return wg.astype(BF16), wu.astype(BF16), wd.astype(BF16)


def _layer_weights(l, p):
    w_in = p["w_in"][l]
    rw0 = 3 * ATT_W
    lo0 = rw0 + 3 * RWKV_W
    row = lambda x: x.reshape(1, -1)
    head_ones = jnp.kron(jnp.eye(N_RWKV_HEADS, dtype=F32), jnp.ones((HEAD_DIM, HEAD_DIM), F32))
    lw = {
        "ffn1": _ffn_weights(p["ffn1_gate"], p["ffn1_up"], p["ffn1_down"]),
        "ffn2": _ffn_weights(p["ffn2_gate"], p["ffn2_up"], p["ffn2_down"]),
        "ffn1_norm": row(p["ffn1_norm"][l]),
        "ffn2_norm": row(p["ffn2_norm"][l]),
        "mix_norm": row(p["mix_norm"][l]),
        "wqkv": w_in[:, :rw0].astype(BF16),
        "wr": w_in[:, rw0:lo0].astype(BF16),
        "wl": jnp.pad(w_in[:, lo0:], ((0, 0), (0, LORA_PAD - LORA_W))).astype(BF16),
        "mu_r": row(p["rwkv_mu"][l][:3 * RWKV_W]),
        "mu_l": row(jnp.pad(p["rwkv_mu"][l][3 * RWKV_W:], (0, LORA_PAD - LORA_W))),
        "w0": row(p["rwkv_w0"][l]),
        "wdec": _pad_rows(p["rwkv_decay_b"][l], LORA_PAD, 0).astype(BF16),
        "a0": row(p["rwkv_a0"][l]),
        "wa": _pad_rows(p["rwkv_a_b"][l], LORA_PAD, DECAY_LORA).astype(BF16),
        "wg": _pad_rows(p["rwkv_g_b"][l], LORA_PAD, DECAY_LORA + AAA_LORA).astype(BF16),
        "k_k": row(p["rwkv_k_k"][l]),
        "k_a": row(p["rwkv_k_a"][l]),
        "r_k": row(p["rwkv_r_k"][l]),
        "gn_w": row(p["rwkv_gn_w"][l]),
        "gn_b": row(p["rwkv_gn_b"][l]),
        "pmat": head_ones.astype(BF16),
        "wo_att": p["w_out"][l][:ATT_W].astype(BF16),
        "wo_rwkv": p["w_out"][l][ATT_W:].astype(BF16),
    }
    if l > 0:
        lw["v0"] = row(p["rwkv_v0"][l - 1])
        lw["v_a"] = jnp.pad(p["rwkv_v_a"][l - 1], ((0, 0), (0, MV_PAD - MV_LORA))).astype(BF16)
        lw["v_b"] = _pad_rows(p["rwkv_v_b"][l - 1], MV_PAD, 0).astype(BF16)
    return lw


def _split_shift(shift):
    return shift[:, :3 * RWKV_W], jnp.pad(shift[:, 3 * RWKV_W:], ((0, 0), (0, LORA_PAD - LORA_W)))


def _run(x, layers, final_g, tabs, tm, mix):
    h = x
    v_first = None
    new_k, new_v, new_wkv, new_shift = [], [], [], []
    depth = len(layers)
    for l, lw in enumerate(layers):
        h = _ffn(h, lw["ffn1_norm"], l, *lw["ffn1"], tm)
        q, k, v, zr, zl = _inproj(h, lw["mix_norm"], lw["wqkv"], lw["wr"], lw["wl"], tabs, tm)
        o_att, (r, kt, vv, g, y), wkv_t, keep_k, keep_v, shift_t = mix(l, lw, q, k, v, zr, zl, v_first)
        if l == 0:
            v_first = vv
        h = _outproj(h, o_att, y, r, kt, vv, g, lw, tm)
        h = _ffn(h, lw["ffn2_norm"], l, *lw["ffn2"], tm, final_g if l == depth - 1 else None)
        new_k.append(keep_k)
        new_v.append(keep_v)
        new_wkv.append(wkv_t)
        new_shift.append(shift_t)
    return h, jnp.stack(new_k), jnp.stack(new_v), jnp.stack(new_wkv), jnp.stack(new_shift)


def kernel(x_prompt, x_sample, cache_k, cache_v, state_wkv, state_shift, ffn1_norm, ffn1_gate, ffn1_up, ffn1_down, mix_norm, w_in, rwkv_mu, rwkv_w0, rwkv_decay_b, rwkv_a0, rwkv_a_b, rwkv_g_b, rwkv_k_k, rwkv_k_a, rwkv_r_k, rwkv_gn_w, rwkv_gn_b, rwkv_v0, rwkv_v_a, rwkv_v_b, w_out, ffn2_norm, ffn2_gate, ffn2_up, ffn2_down, final_norm):
    p = dict(ffn1_norm=ffn1_norm, ffn1_gate=ffn1_gate, ffn1_up=ffn1_up, ffn1_down=ffn1_down,
             mix_norm=mix_norm, w_in=w_in, rwkv_mu=rwkv_mu, rwkv_w0=rwkv_w0,
             rwkv_decay_b=rwkv_decay_b, rwkv_a0=rwkv_a0, rwkv_a_b=rwkv_a_b, rwkv_g_b=rwkv_g_b,
             rwkv_k_k=rwkv_k_k, rwkv_k_a=rwkv_k_a, rwkv_r_k=rwkv_r_k, rwkv_gn_w=rwkv_gn_w,
             rwkv_gn_b=rwkv_gn_b, rwkv_v0=rwkv_v0, rwkv_v_a=rwkv_v_a, rwkv_v_b=rwkv_v_b,
             w_out=w_out, ffn2_norm=ffn2_norm, ffn2_gate=ffn2_gate, ffn2_up=ffn2_up,
             ffn2_down=ffn2_down)
    depth = w_in.shape[0]
    layers = [_layer_weights(l, p) for l in range(depth)]
    final_g = final_norm.reshape(1, -1)
    d_model = x_prompt.shape[-1]
    cols = 3 * RWKV_W + LORA_W

    def shift_out(zr, zl, batch):
        last = lambda z: z.reshape(batch, -1, z.shape[-1])[:, -1]
        return jnp.concatenate([last(zr), last(zl)[:, :LORA_W]], axis=-1)

    bp, sp, _ = x_prompt.shape
    keep = min(MAX_WINDOW, sp)
    tm_p = 512
    zero_shift = _split_shift(jnp.zeros((bp, cols), F32))
    zero_wkv = jnp.zeros((bp, N_RWKV_HEADS, HEAD_DIM, HEAD_DIM), F32)

    def heads(t, width):
        window = t.reshape(bp, sp, width)[:, sp - keep:]
        return window.reshape(bp, keep, width // HEAD_DIM, HEAD_DIM)

    def prompt_mix(l, lw, q, k, v, zr, zl, v_first):
        o_att = _attn_prompt(q, k, v, bp, sp)
        r, lwd, kt, vv, av, bv, g = _prep(zr, zl, lw, tm_p, shift0=zero_shift, seq=sp,
                                          v_first=v_first if l > 0 else None)
        y, wkv_t = _wkv_prompt(r, lwd, kt, vv, av, bv, zero_wkv, bp, sp, tm_p)
        return (o_att, (r, kt, vv, g, y), wkv_t, heads(k, ATT_W), heads(v, ATT_W),
                shift_out(zr, zl, bp))

    y_p, k_p, v_p, wkv_p, shift_p = _run(
        x_prompt.reshape(bp * sp, d_model), layers, final_g, _rope_tables(jnp.arange(sp)), tm_p,
        prompt_mix)

    bs, ts, _ = x_sample.shape
    assert ts == 1
    pos_s = jnp.tile(PAST_LEN + jnp.arange(ts), bs)

    def sample_mix(l, lw, q, k, v, zr, zl, v_first):
        o_att = _attn_sample(l, q, k, v, cache_k, cache_v)
        r, lwd, kt, vv, av, bv, g = _prep(zr, zl, lw, bs, prev_rows=_split_shift(state_shift[l]),
                                          v_first=v_first if l > 0 else None)
        y, wkv_t = _wkv_step(r, lwd, kt, vv, av, bv, state_wkv[l], SUBLANES)
        new_rows = lambda t: t.reshape(bs, ts, N_ATT_HEADS, HEAD_DIM)
        return (o_att, (r, kt, vv, g, y), wkv_t, new_rows(k), new_rows(v),
                shift_out(zr, zl, bs))

    y_s, k_s, v_s, wkv_s, shift_s = _run(
        x_sample.reshape(bs * ts, d_model), layers, final_g, _rope_tables(pos_s), bs * ts,
        sample_mix)

    return (y_p.reshape(bp, sp, d_model), y_s.reshape(bs, ts, d_model), k_p, v_p, wkv_p, shift_p,
            k_s, v_s, wkv_s, shift_s)
```

```python
import functools

import jax
import jax.numpy as jnp
from jax import lax
from jax.experimental import pallas as pl
from jax.experimental.pallas import tpu as pltpu

F32 = jnp.float32
BF16 = jnp.bfloat16

HEAD_DIM = 64
N_ATT_HEADS = 8
N_RWKV_HEADS = 8
ATT_W = N_ATT_HEADS * HEAD_DIM
RWKV_W = N_RWKV_HEADS * HEAD_DIM
BRANCHES = ((128, 1), (512, 4), (2048, 16))
MAX_WINDOW = 2048
Q_BLK = 128
ROPE_THETA = 10000.0
DECAY_LORA = 64
AAA_LORA = 64
MV_LORA = 32
GATE_LORA = 160
LORA_W = DECAY_LORA + AAA_LORA + GATE_LORA
PAST_LEN = 8192
RMS_EPS = 1e-6
GN_EPS = 64e-5

LANES = 128
SUBLANES = 8
LORA_PAD = -(-LORA_W // LANES) * LANES
MV_PAD = LANES
HEADS_PER_VREG = LANES // HEAD_DIM
N_HEAD_PAIRS = RWKV_W // LANES
VMEM_LIMIT = 56 * 1024 * 1024
FF_CHUNK = 256
WKV_CHUNK = 64
ATT_TILE = MAX_WINDOW
ATT_UNROLL = 8


def _dot(a, b):
    return jnp.dot(a, b, preferred_element_type=F32)


def _dot_nt(a, b):
    return lax.dot_general(a, b, (((1,), (1,)), ((), ())), preferred_element_type=F32)


def _bdot(a, b):
    return _dot(a.astype(BF16), b.astype(BF16))


def _split_dot(x, p):
    hi = x.astype(BF16)
    lo = (x - hi.astype(F32)).astype(BF16)
    return _dot(hi, p) + _dot(lo, p)


def _rms(x, g):
    return x * lax.rsqrt(jnp.mean(x * x, -1, keepdims=True) + RMS_EPS) * g


def _params(*sem):
    return pltpu.CompilerParams(dimension_semantics=sem, vmem_limit_bytes=VMEM_LIMIT)


def _const_spec(shape):
    nd = len(shape)
    return pl.BlockSpec(shape, lambda *_: (0,) * nd, pipeline_mode=pl.Buffered(1))


def _ffn_body(h_ref, g_ref, wg_ref, wu_ref, wd_ref, *rest, final):
    if final:
        fg_ref, o_ref, n_s, acc_s = rest
    else:
        o_ref, n_s, acc_s = rest
    x = h_ref[...]
    n_s[...] = _rms(x, g_ref[...]).astype(BF16)
    acc_s[...] = jnp.zeros_like(acc_s)

    for c in range(wg_ref.shape[1] // FF_CHUNK):
        cols = slice(c * FF_CHUNK, (c + 1) * FF_CHUNK)
        n = n_s[...]
        gt = _dot(n, wg_ref[:, cols])
        up = _dot(n, wu_ref[:, cols])
        act = (gt * jax.nn.sigmoid(gt) * up).astype(BF16)
        acc_s[...] += _dot(act, wd_ref[cols, :])
    y = x + 0.5 * acc_s[...]
    if final:
        y = _rms(y, fg_ref[...])
    o_ref[...] = y


def _layer_spec(stacked, l):
    return pl.BlockSpec((None,) + stacked.shape[1:], lambda *_: (l, 0, 0),
                        pipeline_mode=pl.Buffered(1))


def _ffn(h, g, l, wg_all, wu_all, wd_all, tm, final_g=None):
    n, d = h.shape
    row = pl.BlockSpec((tm, d), lambda i: (i, 0))
    in_specs = [row, _const_spec((1, d)), _layer_spec(wg_all, l), _layer_spec(wu_all, l),
                _layer_spec(wd_all, l)]
    args = [h, g, wg_all, wu_all, wd_all]
    if final_g is not None:
        in_specs.append(_const_spec((1, d)))
        args.append(final_g)
    return pl.pallas_call(
        functools.partial(_ffn_body, final=final_g is not None),
        grid=(n // tm,),
        in_specs=in_specs,
        out_specs=row,
        out_shape=jax.ShapeDtypeStruct((n, d), F32),
        scratch_shapes=[pltpu.VMEM((tm, d), BF16), pltpu.VMEM((tm, d), F32)],
        compiler_params=_params("parallel"),
        name="ffn",
    )(*args)


def _rope(x, c, s1, s2):
    reps = x.shape[1] // LANES
    c, s1, s2 = (jnp.concatenate([t] * reps, axis=1) for t in (c, s1, s2))
    half = HEAD_DIM // 2
    upper = pltpu.roll(x, x.shape[1] - half, axis=1)
    lower = pltpu.roll(x, half, axis=1)
    return x * c + upper * s1 + lower * s2


def _inproj_body(h_ref, g_ref, wqkv_ref, wr_ref, wl_ref, c_ref, s1_ref, s2_ref,
                 q_ref, k_ref, v_ref, zr_ref, zl_ref):
    n = _rms(h_ref[...], g_ref[...]).astype(BF16)
    c, s1, s2 = c_ref[...], s1_ref[...], s2_ref[...]
    q = _dot(n, wqkv_ref[:, 0:ATT_W])
    q_ref[...] = _rope(q, c, s1, s2) * (HEAD_DIM ** -0.5)
    k = _dot(n, wqkv_ref[:, ATT_W:2 * ATT_W])
    k_ref[...] = _rope(k, c, s1, s2)
    v_ref[...] = _dot(n, wqkv_ref[:, 2 * ATT_W:3 * ATT_W])
    zr_ref[...] = _dot(n, wr_ref[...])
    zl_ref[...] = _dot(n, wl_ref[...])


def _inproj(h, g, wqkv, wr, wl, tabs, tm):
    n, d = h.shape
    tab_blocks = tabs[0].shape[0] // tm
    row = lambda w: pl.BlockSpec((tm, w), lambda i: (i, 0))
    tab = pl.BlockSpec((tm, LANES), lambda i: (i % tab_blocks, 0))
    widths = (ATT_W, ATT_W, ATT_W, 3 * RWKV_W, LORA_PAD)
    return pl.pallas_call(
        _inproj_body,
        grid=(n // tm,),
        in_specs=[row(d), _const_spec((1, d)), _const_spec(wqkv.shape), _const_spec(wr.shape),
                  _const_spec(wl.shape), tab, tab, tab],
        out_specs=[row(w) for w in widths],
        out_shape=[jax.ShapeDtypeStruct((n, w), F32) for w in widths],
        compiler_params=_params("parallel"),
        name="inproj_rope",
    )(h, g, wqkv, wr, wl, *tabs)


def _attn_body(q_ref, kp_ref, kc_ref, vp_ref, vc_ref, o_ref, ks, vs, acc_s, m_s, den_s, *, tq):
    t = pl.program_id(2)
    ks[0:tq, :] = kp_ref[...]
    ks[tq:2 * tq, :] = kc_ref[...]
    vs[0:tq, :] = vp_ref[...]
    vs[tq:2 * tq, :] = vc_ref[...]
    lane = lax.broadcasted_iota(jnp.int32, (1, LANES), 1)
    head_a = lane < HEAD_DIM
    row = lax.broadcasted_iota(jnp.int32, (Q_BLK, 2 * Q_BLK), 0)
    col = lax.broadcasted_iota(jnp.int32, (Q_BLK, 2 * Q_BLK), 1)

    for bi, (window, dil) in enumerate(BRANCHES):
        n_steps = window // dil
        band = (col >= row + (Q_BLK - n_steps)) & (col <= row + Q_BLK)
        span = Q_BLK * dil

        def block(idx, bi=bi, dil=dil, band=band, span=span):
            sb = idx // dil
            res = idx - sb * dil
            qbase = sb * span + res
            if dil == 1:
                qsl = pl.ds(pl.multiple_of(qbase, Q_BLK), Q_BLK)
                ksl = pl.ds(pl.multiple_of(tq + qbase - span, Q_BLK), 2 * Q_BLK)
            else:
                qsl = pl.ds(qbase, Q_BLK, stride=dil)
                ksl = pl.ds(tq + qbase - span, 2 * Q_BLK, stride=dil)
            q = q_ref[qsl, :]
            kb = ks[ksl, :].astype(BF16)
            vb = vs[ksl, :].astype(BF16)
            cmin = jnp.where(jnp.logical_and(t == 0, sb == 0), Q_BLK, 0)
            mask = band & (col >= cmin)
            scores = []
            for hm in (head_a, jnp.logical_not(head_a)):
                scores.append(_dot_nt(jnp.where(hm, q, 0.0).astype(BF16), kb))
                yield
            parts = []
            for s in scores:
                s = jnp.where(mask, s, -jnp.inf)
                m = jnp.max(s, -1, keepdims=True)
                p = jnp.exp(s - m)
                den = jnp.sum(p, -1, keepdims=True)
                parts.append((_dot(p.astype(BF16), vb), m, den))
                yield
            (pa, ma, da), (pb, mb, db) = parts
            acc_s[bi, qsl, :] = jnp.where(head_a, pa, pb)
            m_s[bi, qsl, :] = jnp.where(head_a, ma, mb)
            den_s[bi, qsl, :] = jnp.where(head_a, da, db)

        def body(i, carry, block=block):
            _lockstep([block(i * ATT_UNROLL + u) for u in range(ATT_UNROLL)])
            return carry

        lax.fori_loop(0, tq // Q_BLK // ATT_UNROLL, body, 0)

    m_all = [m_s[i] for i in range(len(BRANCHES))]
    m_max = functools.reduce(jnp.maximum, m_all)
    wts = [jnp.exp(m - m_max) for m in m_all]
    num = sum(w * acc_s[i] for i, w in enumerate(wts))
    tot = sum(w * den_s[i] for i, w in enumerate(wts))
    o_ref[...] = num / tot


def _attn_prompt(q, k, v, batch, seq):
    n, w = q.shape
    tq = ATT_TILE
    assert seq % tq == 0
    nt = seq // tq
    cur = pl.BlockSpec((tq, LANES), lambda b, hp, t: (b * nt + t, hp))
    prev = pl.BlockSpec((tq, LANES), lambda b, hp, t: (b * nt + jnp.maximum(t - 1, 0), hp))
    nb = len(BRANCHES)
    return pl.pallas_call(
        functools.partial(_attn_body, tq=tq),
        grid=(batch, w // LANES, nt),
        in_specs=[cur, prev, cur, prev, cur],
        out_specs=cur,
        out_shape=jax.ShapeDtypeStruct((n, w), F32),
        scratch_shapes=[pltpu.VMEM((2 * tq, LANES), F32), pltpu.VMEM((2 * tq, LANES), F32),
                        pltpu.VMEM((nb, tq, LANES), F32), pltpu.VMEM((nb, tq, LANES), F32),
                        pltpu.VMEM((nb, tq, LANES), F32)],
        compiler_params=_params("parallel", "parallel", "arbitrary"),
        name="attn_prompt",
    )(q, k, k, v, v)


def _attn_sample_body(qt_ref, knt_ref, vnt_ref, kt_ref, vt_ref, o_ref):
    qt, knt, vnt = qt_ref[0], knt_ref[0], vnt_ref[0]
    n_heads, _, buf_len = kt_ref.shape
    pos = lax.broadcasted_iota(jnp.int32, (1, buf_len), 1)
    dist = buf_len - pos
    cnt = sum((jnp.logical_and(lax.rem(dist, dil) == 0, dist <= window)).astype(F32)
              for window, dil in BRANCHES)
    live = cnt > 0.0
    head_lane = lax.broadcasted_iota(jnp.int32, (1, n_heads), 1)
    s_new_all = jnp.sum(qt * knt, 0, keepdims=True)
    out = jnp.zeros_like(qt)
    for h in range(n_heads):
        pick = lambda x: jnp.sum(jnp.where(head_lane == h, x, 0.0), 1, keepdims=True)
        s = jnp.sum(pick(qt) * kt_ref[h], 0, keepdims=True)
        s_new = pick(s_new_all)
        m = jnp.maximum(jnp.max(jnp.where(live, s, -jnp.inf), -1, keepdims=True), s_new)
        c = jnp.where(live, cnt * jnp.exp(s - m), 0.0)
        c_new = len(BRANCHES) * jnp.exp(s_new - m)
        tot = jnp.sum(c, -1, keepdims=True) + c_new
        num = jnp.sum(c * vt_ref[h], -1, keepdims=True) + c_new * pick(vnt)
        out = jnp.where(head_lane == h, num / tot, out)
    o_ref[0] = out


def _attn_sample(l, q, kn, vn, cache_k, cache_v):
    depth, b, buf_len, h, e = cache_k.shape
    assert all(buf_len >= window for window, _ in BRANCHES)
    by_pos = lambda c: jnp.transpose(c, (0, 1, 3, 4, 2))
    cache_spec = pl.BlockSpec((None, None, h, e, buf_len), lambda i: (l, i, 0, 0, 0))
    row = pl.BlockSpec((1, e, h), lambda i: (i, 0, 0))
    cols = lambda t: t.reshape(b, h, e).transpose(0, 2, 1)
    out = pl.pallas_call(
        _attn_sample_body,
        grid=(b,),
        in_specs=[row, row, row, cache_spec, cache_spec],
        out_specs=row,
        out_shape=jax.ShapeDtypeStruct((b, e, h), F32),
        compiler_params=_params("parallel"),
        name="attn_sample",
    )(cols(q), cols(kn), cols(vn), by_pos(cache_k), by_pos(cache_v))
    return out.transpose(0, 2, 1).reshape(b, h * e)


def _prep_body(*refs, has_vres, rowwise_prev, tiles_per_seq):
    it = iter(refs)
    zr_ref, zl_ref = next(it), next(it)
    if rowwise_prev:
        zpr, zpl = next(it)[...], next(it)[...]
        zr, zl = zr_ref[...], zl_ref[...]
    else:
        pr8_ref, pl8_ref, s0r_ref, s0l_ref = next(it), next(it), next(it), next(it)
        zr, zl = zr_ref[...], zl_ref[...]
        is_start = (pl.program_id(0) % tiles_per_seq) == 0
        row0 = lax.broadcasted_iota(jnp.int32, (zr.shape[0], 1), 0) == 0
        last = SUBLANES - 1
        prev_r = jnp.where(is_start, s0r_ref[0], pr8_ref[last:last + 1, :])
        prev_l = jnp.where(is_start, s0l_ref[0], pl8_ref[last:last + 1, :])
        zpr = jnp.where(row0, prev_r, pltpu.roll(zr, 1, axis=0))
        zpl = jnp.where(row0, prev_l, pltpu.roll(zl, 1, axis=0))
    mur, mul, w0, wdec, a0, wa, wg, k_k, k_a, pmat = (next(it) for _ in range(10))
    if has_vres:
        vf_ref, v0, va, vb = next(it), next(it), next(it), next(it)
    r_o, lw_o, kt_o, v_o, av_o, bv_o, g_o = (next(it) for _ in range(7))

    zmr = zr + (zpr - zr) * mur[...]
    zml = zl + (zpl - zl) * mul[...]
    r = zmr[:, 0:RWKV_W]
    k = zmr[:, RWKV_W:2 * RWKV_W]
    v = zmr[:, 2 * RWKV_W:3 * RWKV_W]
    y = w0[...] + _dot(jnp.tanh(zml).astype(BF16), wdec[...])
    w_log = -(jnp.maximum(-y, 0.0) + jnp.log1p(jnp.exp(-jnp.abs(y)))) - 0.5
    lw_o[...] = -jnp.exp(w_log)
    a = jax.nn.sigmoid(a0[...] + _dot(zml.astype(BF16), wa[...]))
    g_o[...] = _dot(jax.nn.sigmoid(zml).astype(BF16), wg[...])
    if has_vres:
        mix = jax.nn.sigmoid(v0[...] + _dot(_dot(v.astype(BF16), va[...]).astype(BF16), vb[...]))
        v = v + (vf_ref[...] - v) * mix
    kk = k * k_k[...]
    norm = jnp.sqrt(_split_dot(kk * kk, pmat[...]))
    kk = kk / jnp.maximum(norm, 1e-12)
    r_o[...] = r
    kt_o[...] = k * (1.0 + (a - 1.0) * k_a[...])
    v_o[...] = v
    av_o[...] = -kk
    bv_o[...] = kk * a


def _prep(zr, zl, lw, tm, *, prev_rows=None, shift0=None, seq=None, v_first=None):
    n = zr.shape[0]
    row = lambda w: pl.BlockSpec((tm, w), lambda i: (i, 0))
    args, in_specs = [zr, zl], [row(3 * RWKV_W), row(LORA_PAD)]
    if prev_rows is not None:
        args += list(prev_rows)
        in_specs += [row(3 * RWKV_W), row(LORA_PAD)]
        tiles_per_seq = None
    else:
        tiles_per_seq = seq // tm
        t8 = tm // SUBLANES
        prev8 = lambda w: pl.BlockSpec((SUBLANES, w), lambda i: (jnp.maximum(i * t8 - 1, 0), 0))
        start = lambda w: pl.BlockSpec((1, 1, w), lambda i: (i // tiles_per_seq, 0, 0))
        args += [zr, zl, shift0[0][:, None], shift0[1][:, None]]
        in_specs += [prev8(3 * RWKV_W), prev8(LORA_PAD), start(3 * RWKV_W), start(LORA_PAD)]
    consts = [lw["mu_r"], lw["mu_l"], lw["w0"], lw["wdec"], lw["a0"], lw["wa"], lw["wg"],
              lw["k_k"], lw["k_a"], lw["pmat"]]
    args += consts
    in_specs += [_const_spec(c.shape) for c in consts]
    if v_first is not None:
        vres = [lw["v0"], lw["v_a"], lw["v_b"]]
        args += [v_first] + vres
        in_specs += [row(RWKV_W)] + [_const_spec(c.shape) for c in vres]
    return pl.pallas_call(
        functools.partial(_prep_body, has_vres=v_first is not None,
                          rowwise_prev=prev_rows is not None, tiles_per_seq=tiles_per_seq),
        grid=(n // tm,),
        in_specs=in_specs,
        out_specs=[row(RWKV_W)] * 7,
        out_shape=[jax.ShapeDtypeStruct((n, RWKV_W), F32)] * 7,
        compiler_params=_params("parallel"),
        name="rwkv_premix",
    )(*args)


def _cumsum_rows(x):
    rows = lax.broadcasted_iota(jnp.int32, (x.shape[0], 1), 0)
    s = 1
    while s < x.shape[0]:
        x = x + jnp.where(rows >= s, pltpu.roll(x, s, axis=0), 0.0)
        s *= 2
    return x


def _unit_lower_inverse(lmat, ri, ci):
    blk = lambda sh: (ri >> sh) == (ci >> sh)
    same16, same32 = blk(4), blk(5)
    eye = (ri == ci).astype(F32)
    n = lmat.shape[0]
    x = jnp.where(same16, lmat, 0.0)
    p = eye + x
    x = _bdot(x, x)
    yield
    for _ in range(2):
        xp = _bdot(x, jnp.concatenate([x, p], axis=1))
        yield
        x, p = xp[:, :n], p + xp[:, n:]
    t = p + _bdot(x, p)
    yield
    for off in (jnp.where(jnp.logical_and(same32, jnp.logical_not(same16)), lmat, 0.0),
                jnp.where(same32, 0.0, lmat)):
        ot = _bdot(off, t)
        yield
        t = t + _bdot(t, ot)
        yield
    return t


def _lockstep(gens):
    while gens:
        alive = []
        for g in gens:
            try:
                next(g)
                alive.append(g)
            except StopIteration:
                pass
        gens = alive


def _wkv_body(r_ref, lw_ref, kt_ref, v_ref, av_ref, bv_ref, s0_ref, y_ref, st_ref, st_s, *, chunk):
    t = pl.program_id(0)

    @pl.when(t == 0)
    def _():
        st_s[...] = s0_ref[...]

    lane = lax.broadcasted_iota(jnp.int32, (1, LANES), 1)
    head_a = lane < HEAD_DIM
    n2 = HEADS_PER_VREG * chunk
    ri = lax.broadcasted_iota(jnp.int32, (n2, n2), 0)
    ci = lax.broadcasted_iota(jnp.int32, (n2, n2), 1)
    strict = ci < ri
    incl = ci <= ri

    def pstack(x):
        return jnp.concatenate([jnp.where(head_a, x, 0.0), jnp.where(head_a, 0.0, x)], axis=0)

    def advance(b, hp, sl):
        cols = slice(hp * LANES, (hp + 1) * LANES)
        r, lw, kt, v, av, bv = (ref[b, sl, cols] for ref in (r_ref, lw_ref, kt_ref, v_ref, av_ref, bv_ref))
        lc = _cumsum_rows(lw)
        lc_last = lc[chunk - 1:chunk, :]
        g_inv = jnp.exp(-lc)
        g_rem = jnp.exp(lc_last - lc)
        lhs = jnp.concatenate([pstack(av * jnp.exp(lc - lw)), pstack(r * jnp.exp(lc))], axis=0).astype(BF16)
        rhs = jnp.concatenate([pstack(bv * g_inv), pstack(kt * g_inv)], axis=0).astype(BF16)
        sc = _dot_nt(lhs, rhs)
        yield
        l_ab = jnp.where(strict, sc[0:n2, 0:n2], 0.0)
        l_ak = jnp.where(strict, sc[0:n2, n2:], 0.0)
        a_rb = jnp.where(incl, sc[n2:, 0:n2], 0.0)
        a_rk = jnp.where(incl, sc[n2:, n2:], 0.0)
        v2 = pstack(v)
        akv = _bdot(l_ak, v2)
        yield
        t_inv = yield from _unit_lower_inverse(l_ab, ri, ci)

        state = st_s[b, hp]
        hs = _dot_nt(lhs, state.astype(BF16))
        yield
        u2 = _bdot(t_inv, hs[0:n2] + akv)
        yield
        uv = jnp.concatenate([u2, v2], axis=0)
        bk = jnp.concatenate([pstack(bv * g_rem), pstack(kt * g_rem)], axis=0)
        st_s[b, hp] = state * jnp.exp(lc_last) + _bdot(uv.T, bk)
        yield
        y2 = hs[n2:] + _bdot(jnp.concatenate([a_rb, a_rk], axis=1), uv)
        y_ref[b, sl, cols] = y2[0:chunk] + y2[chunk:]

    def body(c, carry):
        sl = pl.ds(pl.multiple_of(c * chunk, chunk), chunk)
        _lockstep([advance(b, hp, sl) for b in range(r_ref.shape[0])
                   for hp in range(r_ref.shape[2] // LANES)])
        return carry

    lax.fori_loop(0, r_ref.shape[1] // chunk, body, 0)

    @pl.when(t == pl.num_programs(0) - 1)
    def _():
        st_ref[...] = st_s[...]


def _pair_state(s):
    b, h, n, _ = s.shape
    s = s.reshape(b, h // HEADS_PER_VREG, HEADS_PER_VREG, n, n)
    eye = jnp.eye(HEADS_PER_VREG, dtype=s.dtype)
    return jnp.einsum("bpaij,ac->bpaicj", s, eye).reshape(
        b, h // HEADS_PER_VREG, HEADS_PER_VREG * n, HEADS_PER_VREG * n)


def _unpair_state(s2):
    b, hp, n2, _ = s2.shape
    n = n2 // HEADS_PER_VREG
    s = s2.reshape(b, hp, HEADS_PER_VREG, n, HEADS_PER_VREG, n)
    s = jnp.stack([s[:, :, a, :, a, :] for a in range(HEADS_PER_VREG)], axis=2)
    return s.reshape(b, hp * HEADS_PER_VREG, n, n)


def _wkv_prompt(r, lw, kt, v, av, bv, s0, batch, seq, tb):
    n, w = r.shape
    s2 = _pair_state(s0)
    tok = pl.BlockSpec((batch, tb, w), lambda t: (0, t, 0))
    st = pl.BlockSpec(s2.shape, lambda t: (0, 0, 0, 0))
    seqs = lambda x: x.reshape(batch, seq, w)
    y, s_t = pl.pallas_call(
        functools.partial(_wkv_body, chunk=WKV_CHUNK),
        grid=(seq // tb,),
        in_specs=[tok] * 6 + [st],
        out_specs=[tok, st],
        out_shape=[jax.ShapeDtypeStruct((batch, seq, w), F32), jax.ShapeDtypeStruct(s2.shape, F32)],
        scratch_shapes=[pltpu.VMEM(s2.shape, F32)],
        compiler_params=_params("arbitrary"),
        name="wkv_chunked",
    )(seqs(r), seqs(lw), seqs(kt), seqs(v), seqs(av), seqs(bv), s2)
    return y.reshape(n, w), _unpair_state(s_t)


def _wkv_step_body(r_ref, lw_ref, kt_ref, av_ref, bv_ref, v_ref, s_ref, y_ref, so_ref):
    s = s_ref[...]
    sa = jnp.sum(s * av_ref[...], -1, keepdims=True)
    s = s * jnp.exp(lw_ref[...]) + sa * bv_ref[...] + v_ref[...] * kt_ref[...]
    so_ref[...] = s
    y_ref[...] = jnp.sum(s * r_ref[...], -1, keepdims=True)


def _wkv_step(r, lw, kt, v, av, bv, s0, bb):
    b, h, n, _ = s0.shape
    rowv = lambda x: x.reshape(b, h, 1, n)
    key_spec = pl.BlockSpec((bb, h, 1, n), lambda i: (i, 0, 0, 0))
    val_spec = pl.BlockSpec((bb, h, n, 1), lambda i: (i, 0, 0, 0))
    st_spec = pl.BlockSpec((bb, h, n, n), lambda i: (i, 0, 0, 0))
    y, s_t = pl.pallas_call(
        _wkv_step_body,
        grid=(b // bb,),
        in_specs=[key_spec] * 5 + [val_spec, st_spec],
        out_specs=[val_spec, st_spec],
        out_shape=[jax.ShapeDtypeStruct((b, h, n, 1), F32), jax.ShapeDtypeStruct(s0.shape, F32)],
        compiler_params=_params("parallel"),
        name="wkv_step",
    )(rowv(r), rowv(lw), rowv(kt), rowv(av), rowv(bv), v.reshape(b, h, n, 1), s0)
    return y.reshape(b, h * n), s_t


def _outproj_body(h_ref, oa_ref, y_ref, r_ref, kt_ref, v_ref, g_ref, rk_ref, gw_ref, gb_ref,
                  p_ref, wo_ref, o_ref):
    pmat = p_ref[...]
    y = y_ref[...]
    mean = _split_dot(y, pmat) * (1.0 / HEAD_DIM)
    d = y - mean
    var = _split_dot(d * d, pmat) * (1.0 / HEAD_DIM)
    yn = d * lax.rsqrt(var + GN_EPS) * gw_ref[...] + gb_ref[...]
    bonus = _split_dot(r_ref[...] * kt_ref[...] * rk_ref[...], pmat)
    yo = (yn + bonus * v_ref[...]) * g_ref[...]
    o_ref[...] = (h_ref[...] + _dot(oa_ref[...].astype(BF16), wo_ref[0:ATT_W, :])
                  + _dot(yo.astype(BF16), wo_ref[ATT_W:, :]))


def _outproj(h, o_att, y, r, kt, v, g, l, lw, tm):
    n, d = h.shape
    row = lambda w: pl.BlockSpec((tm, w), lambda i: (i, 0))
    consts = [lw["r_k"], lw["gn_w"], lw["gn_b"], lw["pmat"]]
    return pl.pallas_call(
        _outproj_body,
        grid=(n // tm,),
        in_specs=([row(d)] + [row(RWKV_W)] * 6 + [_const_spec(c.shape) for c in consts]
                  + [_layer_spec(lw["w_out"], l)]),
        out_specs=row(d),
        out_shape=jax.ShapeDtypeStruct((n, d), F32),
        compiler_params=_params("parallel"),
        name="outproj",
    )(h, o_att, y, r, kt, v, g, *consts, lw["w_out"])


def _rope_tables(pos):
    half = HEAD_DIM // 2
    inv = ROPE_THETA ** (-jnp.arange(half, dtype=F32) / half)
    ang = pos.astype(F32)[:, None] * inv[None, :]
    cos, sin = jnp.cos(ang), jnp.sin(ang)
    zero = jnp.zeros_like(sin)
    per_head = (jnp.concatenate([cos, cos], 1), jnp.concatenate([-sin, zero], 1),
                jnp.concatenate([zero, sin], 1))
    return tuple(jnp.concatenate([t] * HEADS_PER_VREG, 1) for t in per_head)


def _pad_rows(x, rows, at):
    return jnp.zeros((rows, x.shape[1]), x.dtype).at[at:at + x.shape[0]].set(x)


def _ffn_weights(wg, wu, wd):
    assert wg.shape[2] % FF_CHUNK == 0
    return wg.astype(BF16), wu.astype(BF16), wd.astype(BF16)


def _layer_weights(l, p):
    w_in = p["w_in"][l]
    rw0 = 3 * ATT_W
    lo0 = rw0 + 3 * RWKV_W
    row = lambda x: x.reshape(1, -1)
    head_ones = jnp.kron(jnp.eye(N_RWKV_HEADS, dtype=F32), jnp.ones((HEAD_DIM, HEAD_DIM), F32))
    lw = {
        "ffn1": _ffn_weights(p["ffn1_gate"], p["ffn1_up"], p["ffn1_down"]),
        "ffn2": _ffn_weights(p["ffn2_gate"], p["ffn2_up"], p["ffn2_down"]),
        "ffn1_norm": row(p["ffn1_norm"][l]),
        "ffn2_norm": row(p["ffn2_norm"][l]),
        "mix_norm": row(p["mix_norm"][l]),
        "wqkv": w_in[:, :rw0].astype(BF16),
        "wr": w_in[:, rw0:lo0].astype(BF16),
        "wl": jnp.pad(w_in[:, lo0:], ((0, 0), (0, LORA_PAD - LORA_W))).astype(BF16),
        "mu_r": row(p["rwkv_mu"][l][:3 * RWKV_W]),
        "mu_l": row(jnp.pad(p["rwkv_mu"][l][3 * RWKV_W:], (0, LORA_PAD - LORA_W))),
        "w0": row(p["rwkv_w0"][l]),
        "wdec": _pad_rows(p["rwkv_decay_b"][l], LORA_PAD, 0).astype(BF16),
        "a0": row(p["rwkv_a0"][l]),
        "wa": _pad_rows(p["rwkv_a_b"][l], LORA_PAD, DECAY_LORA).astype(BF16),
        "wg": _pad_rows(p["rwkv_g_b"][l], LORA_PAD, DECAY_LORA + AAA_LORA).astype(BF16),
        "k_k": row(p["rwkv_k_k"][l]),
        "k_a": row(p["rwkv_k_a"][l]),
        "r_k": row(p["rwkv_r_k"][l]),
        "gn_w": row(p["rwkv_gn_w"][l]),
        "gn_b": row(p["rwkv_gn_b"][l]),
        "pmat": head_ones.astype(BF16),
        "w_out": p["w_out"].astype(BF16),
    }
    if l > 0:
        lw["v0"] = row(p["rwkv_v0"][l - 1])
        lw["v_a"] = jnp.pad(p["rwkv_v_a"][l - 1], ((0, 0), (0, MV_PAD - MV_LORA))).astype(BF16)
        lw["v_b"] = _pad_rows(p["rwkv_v_b"][l - 1], MV_PAD, 0).astype(BF16)
    return lw


def _split_shift(shift):
    return shift[:, :3 * RWKV_W], jnp.pad(shift[:, 3 * RWKV_W:], ((0, 0), (0, LORA_PAD - LORA_W)))


def _run(x, layers, final_g, tabs, tm, mix):
    h = x
    v_first = None
    new_k, new_v, new_wkv, new_shift = [], [], [], []
    depth = len(layers)
    for l, lw in enumerate(layers):
        h = _ffn(h, lw["ffn1_norm"], l, *lw["ffn1"], tm)
        q, k, v, zr, zl = _inproj(h, lw["mix_norm"], lw["wqkv"], lw["wr"], lw["wl"], tabs, tm)
        o_att, (r, kt, vv, g, y), wkv_t, keep_k, keep_v, shift_t = mix(l, lw, q, k, v, zr, zl, v_first)
        if l == 0:
            v_first = vv
        h = _outproj(h, o_att, y, r, kt, vv, g, l, lw, tm)
        h = _ffn(h, lw["ffn2_norm"], l, *lw["ffn2"], tm, final_g if l == depth - 1 else None)
        new_k.append(keep_k)
        new_v.append(keep_v)
        new_wkv.append(wkv_t)
        new_shift.append(shift_t)
    return h, jnp.stack(new_k), jnp.stack(new_v), jnp.stack(new_wkv), jnp.stack(new_shift)


def kernel(x_prompt, x_sample, cache_k, cache_v, state_wkv, state_shift, ffn1_norm, ffn1_gate, ffn1_up, ffn1_down, mix_norm, w_in, rwkv_mu, rwkv_w0, rwkv_decay_b, rwkv_a0, rwkv_a_b, rwkv_g_b, rwkv_k_k, rwkv_k_a, rwkv_r_k, rwkv_gn_w, rwkv_gn_b, rwkv_v0, rwkv_v_a, rwkv_v_b, w_out, ffn2_norm, ffn2_gate, ffn2_up, ffn2_down, final_norm):
    p = dict(ffn1_norm=ffn1_norm, ffn1_gate=ffn1_gate, ffn1_up=ffn1_up, ffn1_down=ffn1_down,
             mix_norm=mix_norm, w_in=w_in, rwkv_mu=rwkv_mu, rwkv_w0=rwkv_w0,
             rwkv_decay_b=rwkv_decay_b, rwkv_a0=rwkv_a0, rwkv_a_b=rwkv_a_b, rwkv_g_b=rwkv_g_b,
             rwkv_k_k=rwkv_k_k, rwkv_k_a=rwkv_k_a, rwkv_r_k=rwkv_r_k, rwkv_gn_w=rwkv_gn_w,
             rwkv_gn_b=rwkv_gn_b, rwkv_v0=rwkv_v0, rwkv_v_a=rwkv_v_a, rwkv_v_b=rwkv_v_b,
             w_out=w_out, ffn2_norm=ffn2_norm, ffn2_gate=ffn2_gate, ffn2_up=ffn2_up,
             ffn2_down=ffn2_down)
    depth = w_in.shape[0]
    layers = [_layer_weights(l, p) for l in range(depth)]
    final_g = final_norm.reshape(1, -1)
    d_model = x_prompt.shape[-1]
    cols = 3 * RWKV_W + LORA_W

    def shift_out(zr, zl, batch):
        last = lambda z: z.reshape(batch, -1, z.shape[-1])[:, -1]
        return jnp.concatenate([last(zr), last(zl)[:, :LORA_W]], axis=-1)

    bp, sp, _ = x_prompt.shape
    keep = min(MAX_WINDOW, sp)
    tm_p = 512
    zero_shift = _split_shift(jnp.zeros((bp, cols), F32))
    zero_wkv = jnp.zeros((bp, N_RWKV_HEADS, HEAD_DIM, HEAD_DIM), F32)

    def heads(t, width):
        window = t.reshape(bp, sp, width)[:, sp - keep:]
        return window.reshape(bp, keep, width // HEAD_DIM, HEAD_DIM)

    def prompt_mix(l, lw, q, k, v, zr, zl, v_first):
        o_att = _attn_prompt(q, k, v, bp, sp)
        r, lwd, kt, vv, av, bv, g = _prep(zr, zl, lw, tm_p, shift0=zero_shift, seq=sp,
                                          v_first=v_first if l > 0 else None)
        y, wkv_t = _wkv_prompt(r, lwd, kt, vv, av, bv, zero_wkv, bp, sp, tm_p)
        return (o_att, (r, kt, vv, g, y), wkv_t, heads(k, ATT_W), heads(v, ATT_W),
                shift_out(zr, zl, bp))

    y_p, k_p, v_p, wkv_p, shift_p = _run(
        x_prompt.reshape(bp * sp, d_model), layers, final_g, _rope_tables(jnp.arange(sp)), tm_p,
        prompt_mix)

    bs, ts, _ = x_sample.shape
    assert ts == 1
    pos_s = jnp.tile(PAST_LEN + jnp.arange(ts), bs)

    def sample_mix(l, lw, q, k, v, zr, zl, v_first):
        o_att = _attn_sample(l, q, k, v, cache_k, cache_v)
        r, lwd, kt, vv, av, bv, g = _prep(zr, zl, lw, bs, prev_rows=_split_shift(state_shift[l]),
                                          v_first=v_first if l > 0 else None)
        y, wkv_t = _wkv_step(r, lwd, kt, vv, av, bv, state_wkv[l], SUBLANES)
        new_rows = lambda t: t.reshape(bs, ts, N_ATT_HEADS, HEAD_DIM)
        return (o_att, (r, kt, vv, g, y), wkv_t, new_rows(k), new_rows(v),
                shift_out(zr, zl, bs))

    y_s, k_s, v_s, wkv_s, shift_s = _run(
        x_sample.reshape(bs * ts, d_model), layers, final_g, _rope_tables(pos_s), bs * ts,
        sample_mix)

    return (y_p.reshape(bp, sp, d_model), y_s.reshape(bs, ts, d_model), k_p, v_p, wkv_p, shift_p,
            k_s, v_s, wkv_s, shift_s)
```
